```python
import math
import jax, jax.numpy as jnp
from jax import lax
import numpy as np

D_MODEL = 2048
BATCH = 2
SEQ = 8192
DEPTH = 4

D_MIX = D_MODEL
D_MLSTM = D_MIX // 2
D_HGRN = D_MIX - D_MLSTM
M_HEADS = 4
M_DV = D_MLSTM // M_HEADS
M_DQK = M_DV // 2
H_HEADS = 8
H_DH = D_HGRN // H_HEADS
CONV_W = 4
CHUNK = 64
D_FF = int(math.ceil(8 * D_MODEL / 3 / 256)) * 256
N_MOD = 6
EPS = 1e-5
DEEPNORM_ALPHA = (2 * DEPTH) ** 0.25
DEEPNORM_BETA = (8 * DEPTH) ** -0.25

IN_COLS = (2 * M_HEADS * M_DQK,
           D_MLSTM,
           D_MLSTM,
           M_HEADS,
           M_HEADS,
           D_HGRN,
           D_HGRN,
           D_HGRN,
           D_HGRN)
D_IN = sum(IN_COLS)
SPLIT_IDX = tuple(int(s) for s in np.cumsum(IN_COLS)[:-1])

kernel_name = "hymba_mlstm_hgrn2_deepnorm_adaln"


def layer_norm(x, g, b):
    xf = x.astype(jnp.float32)
    mu = xf.mean(-1, keepdims=True)
    var = jnp.square(xf - mu).mean(-1, keepdims=True)
    return ((xf - mu) * lax.rsqrt(var + EPS)).astype(x.dtype) * g + b


def head_norm(h, w, center):
    hf = h.astype(jnp.float32)
    if center:
        hf = hf - hf.mean(-1, keepdims=True)
    hf = hf * lax.rsqrt(jnp.square(hf).mean(-1, keepdims=True) + EPS)
    return hf.reshape(h.shape[0], h.shape[1], -1) * w


def causal_dwconv(x, w, b):
    T = x.shape[1]
    xp = jnp.pad(x, ((0, 0), (CONV_W - 1, 0), (0, 0)))
    return sum(w[j] * xp[:, j:j + T] for j in range(CONV_W)) + b


def to_chunks(a):
    B, T = a.shape[:2]
    a = a.reshape((B, T // CHUNK, CHUNK) + a.shape[2:])
    return jnp.moveaxis(a, (1, 3), (0, 2))


def from_chunks(a):
    a = jnp.moveaxis(a, (0, 2), (1, 3))
    return a.reshape((a.shape[0], -1) + a.shape[3:])


def mlstm_chunkwise(q, k, v, ig, fg):
    f32 = jnp.float32
    B, T, H, dk = q.shape
    dv = v.shape[-1]
    q = q.astype(f32) * dk ** -0.5
    k = k.astype(f32)
    v = v.astype(f32)
    log_f = jax.nn.log_sigmoid(fg.astype(f32))
    ig = ig.astype(f32)
    causal = jnp.tril(jnp.ones((CHUNK, CHUNK), dtype=bool))

    def step(carry, xs):
        C_prev, n_prev, m_prev = carry
        qc, kc, vc, ic, lfc = xs
        b = jnp.cumsum(lfc, axis=-1)
        logD = jnp.where(causal, b[..., :, None] - b[..., None, :] + ic[..., None, :], -jnp.inf)
        m_inter = b + m_prev[..., None]
        m_t = jnp.maximum(m_inter, logD.max(-1))
        Dw = jnp.exp(logD - m_t[..., None])
        w_inter = jnp.exp(m_inter - m_t)
        S = jnp.einsum('bhtd,bhsd->bhts', qc, kc) * Dw
        num = jnp.einsum('bhts,bhsv->bhtv', S, vc) + w_inter[..., None] * jnp.einsum('bhtd,bhdv->bhtv', qc, C_prev)
        den = S.sum(-1) + w_inter * jnp.einsum('bhtd,bhd->bht', qc, n_prev)
        h = num / jnp.maximum(jnp.abs(den), jnp.exp(-m_t))[..., None]
        b_last = b[..., -1]
        log_w_end = b_last[..., None] - b + ic
        m_new = jnp.maximum(b_last + m_prev, log_w_end.max(-1))
        w_end = jnp.exp(log_w_end - m_new[..., None])
        decay = jnp.exp(b_last + m_prev - m_new)
        C_new = decay[..., None, None] * C_prev + jnp.einsum('bhs,bhsd,bhsv->bhdv', w_end, kc, vc)
        n_new = decay[..., None] * n_prev + jnp.einsum('bhs,bhsd->bhd', w_end, kc)
        return (C_new, n_new, m_new), h

    init = (jnp.zeros((B, H, dk, dv), f32), jnp.zeros((B, H, dk), f32), jnp.zeros((B, H), f32))
    xs = (to_chunks(q), to_chunks(k), to_chunks(v), to_chunks(ig), to_chunks(log_f))
    _, hs = lax.scan(step, init, xs)
    return from_chunks(hs)


def hgrn2_chunkwise(q, f_pre, i, lb):
    f32 = jnp.float32
    B, T, H, dh = q.shape
    q = jax.nn.silu(q.astype(f32))
    f_pre = f_pre.astype(f32)
    log_f = jnp.logaddexp(jnp.log(lb), jnp.log1p(-lb) + jax.nn.log_sigmoid(f_pre))
    k = (1.0 - lb) * jax.nn.sigmoid(-f_pre)
    v = i.astype(f32)
    causal = jnp.tril(jnp.ones((CHUNK, CHUNK), dtype=bool))[:, :, None]

    def step(S_prev, xs):
        qc, kc, vc, lfc = xs
        b = jnp.cumsum(lfc, axis=2)
        rel = jnp.where(causal, b[:, :, :, None, :] - b[:, :, None, :, :], -jnp.inf)
        A = jnp.einsum('bhtd,bhsd,bhtsd->bhts', qc, kc, jnp.exp(rel))
        o = jnp.einsum('bhts,bhsv->bhtv', A, vc) + jnp.einsum('bhtd,bhdv->bhtv', qc * jnp.exp(b), S_prev)
        b_last = b[:, :, -1:, :]
        S_new = jnp.exp(b_last[:, :, 0, :])[..., None] * S_prev + jnp.einsum('bhsd,bhsv->bhdv', kc * jnp.exp(b_last - b), vc)
        return S_new, o

    init = jnp.zeros((B, H, dh, dh), f32)
    xs = (to_chunks(q), to_chunks(k), to_chunks(v), to_chunks(log_f))
    _, os_ = lax.scan(step, init, xs)
    return from_chunks(os_)


def token_mixer(u, w_in, conv_w, conv_b, b_ig, b_fg, mnorm_w, hnorm_w, lb, w_out):
    B, T, _ = u.shape
    proj = u @ w_in
    qk_m, v_m, o_m, ig, fg, q_h, f_h, i_h, g_h = jnp.split(proj, SPLIT_IDX, axis=-1)
    qk_m = jax.nn.silu(causal_dwconv(qk_m, conv_w, conv_b))
    q_m, k_m = jnp.split(qk_m, 2, axis=-1)
    h_m = mlstm_chunkwise(q_m.reshape(B, T, M_HEADS, M_DQK), k_m.reshape(B, T, M_HEADS, M_DQK),
                          v_m.reshape(B, T, M_HEADS, M_DV), ig + b_ig, fg + b_fg)
    y_m = head_norm(h_m, mnorm_w, True) * jax.nn.sigmoid(o_m.astype(jnp.float32))
    shp = (B, T, H_HEADS, H_DH)
    h_h = hgrn2_chunkwise(q_h.reshape(shp), f_h.reshape(shp), i_h.reshape(shp), lb.reshape(H_HEADS, H_DH))
    y_h = head_norm(h_h, hnorm_w, False) * jax.nn.silu(g_h.astype(jnp.float32))
    y = jnp.concatenate([y_m, y_h], axis=-1).astype(u.dtype)
    return y @ w_out


def swiglu(u, w_gate, w_up, w_down):
    return (jax.nn.silu(u @ w_gate) * (u @ w_up)) @ w_down


def setup_inputs(seed: int = 0) -> dict:
    key = jax.random.key(seed)
    ks = jax.random.split(key, 24)
    n = jax.random.normal
    D = D_MODEL
    return {
        "x": n(ks[0], (BATCH, SEQ, D), jnp.float32),
        "c": n(ks[1], (BATCH, D), jnp.float32),
        "w_mod": n(ks[2], (DEPTH, D, N_MOD * D), jnp.float32) * (0.5 * D ** -0.5),
        "b_mod": 0.01 * n(ks[3], (DEPTH, N_MOD * D), jnp.float32),
        "w_in": n(ks[4], (DEPTH, D, D_IN), jnp.float32) * D ** -0.5,
        "conv_w": n(ks[5], (DEPTH, CONV_W, 2 * M_HEADS * M_DQK), jnp.float32) * CONV_W ** -0.5,
        "conv_b": 0.01 * n(ks[6], (DEPTH, 2 * M_HEADS * M_DQK), jnp.float32),
        "b_igate": 0.1 * n(ks[7], (DEPTH, M_HEADS), jnp.float32),
        "b_fgate": jnp.linspace(3.0, 6.0, M_HEADS, dtype=jnp.float32) + 0.1 * n(ks[8], (DEPTH, M_HEADS), jnp.float32),
        "mlstm_norm_w": 1.0 + 0.02 * n(ks[9], (DEPTH, D_MLSTM), jnp.float32),
        "hgrn_norm_w": 1.0 + 0.02 * n(ks[10], (DEPTH, D_HGRN), jnp.float32),
        "lb_logits": 0.1 * n(ks[11], (DEPTH, D_HGRN), jnp.float32),
        "w_out": n(ks[12], (DEPTH, D_MIX, D), jnp.float32) * (D_MIX ** -0.5 * DEEPNORM_BETA),
        "ln1_g": 1.0 + 0.02 * n(ks[13], (DEPTH, D), jnp.float32),
        "ln1_b": 0.01 * n(ks[14], (DEPTH, D), jnp.float32),
        "w_gate": n(ks[15], (DEPTH, D, D_FF), jnp.float32) * D ** -0.5,
        "w_up": n(ks[16], (DEPTH, D, D_FF), jnp.float32) * D ** -0.5,
        "w_down": n(ks[17], (DEPTH, D_FF, D), jnp.float32) * (D_FF ** -0.5 * DEEPNORM_BETA),
        "ln2_g": 1.0 + 0.02 * n(ks[18], (DEPTH, D), jnp.float32),
        "ln2_b": 0.01 * n(ks[19], (DEPTH, D), jnp.float32),
    }


def reference(x, c, w_mod, b_mod, w_in, conv_w, conv_b, b_igate, b_fgate, mlstm_norm_w, hgrn_norm_w,
              lb_logits, w_out, ln1_g, ln1_b, w_gate, w_up, w_down, ln2_g, ln2_b):
    p = jax.nn.softmax(lb_logits.astype(jnp.float32), axis=0)
    cs = jnp.cumsum(p, axis=0)
    lb_all = cs - cs[0:1]
    c_act = jax.nn.silu(c)
    for l in range(DEPTH):
        mod = c_act @ w_mod[l] + b_mod[l]
        sh1, sc1, g1, sh2, sc2, g2 = jnp.split(mod[:, None, :], N_MOD, axis=-1)
        u = x * (1.0 + sc1) + sh1
        y = token_mixer(u, w_in[l], conv_w[l], conv_b[l], b_igate[l], b_fgate[l],
                        mlstm_norm_w[l], hgrn_norm_w[l], lb_all[l], w_out[l])
        x = layer_norm(DEEPNORM_ALPHA * x + (1.0 + g1) * y, ln1_g[l], ln1_b[l])
        u = x * (1.0 + sc2) + sh2
        y = swiglu(u, w_gate[l], w_up[l], w_down[l])
        x = layer_norm(DEEPNORM_ALPHA * x + (1.0 + g2) * y, ln2_g[l], ln2_b[l])
    return x
```

```python
import functools
import math

import numpy as np
import jax
import jax.numpy as jnp
from jax import lax
from jax.experimental import pallas as pl
from jax.experimental.pallas import tpu as pltpu

F32 = jnp.float32
BF16 = jnp.bfloat16

D_MODEL = 2048
DEPTH = 4
D_MLSTM = 1024
D_HGRN = 1024
M_HEADS = 4
M_DV = 256
M_DQK = 128
H_HEADS = 8
H_DH = 128
CONV_W = 4
D_FF = 5632
N_MOD = 6
EPS = 1e-5
ALPHA = (2 * DEPTH) ** 0.25
N_GATE = 128
CONV_HALO = 8

VMEM_LIMIT = 56 * 1024 * 1024

MLSTM_CHUNK = 256
HGRN_CHUNK = 64


def _cparams(sem):
    return pltpu.CompilerParams(dimension_semantics=sem, vmem_limit_bytes=VMEM_LIMIT)


def _sigmoid(x):
    return 1.0 / (1.0 + jnp.exp(-x))


def _log_sigmoid(x):
    return jnp.minimum(x, 0.0) - jnp.log1p(jnp.exp(-jnp.abs(x)))


def _split3(x):
    hi = x.astype(BF16)
    r1 = x - hi.astype(F32)
    mid = r1.astype(BF16)
    lo = (r1 - mid.astype(F32)).astype(BF16)
    return hi, mid, lo


def _dot(a, b):
    return jnp.dot(a, b, preferred_element_type=F32)


def _dot_nt(a, b):
    return lax.dot_general(a, b, (((1,), (1,)), ((), ())), preferred_element_type=F32)


def _dot_tn(a, b):
    return lax.dot_general(a, b, (((0,), (0,)), ((), ())), preferred_element_type=F32)


def _mod_kernel(c_ref, w_ref, b_ref, o_ref):
    c = c_ref[...]
    ca = (c * _sigmoid(c)).astype(BF16)
    o_ref[0] = _dot(ca, w_ref[0].astype(BF16)) + b_ref[0]


def _mod_all(c_pad, w_mod, b_mod):
    depth, d, n = w_mod.shape
    tn = 1024
    rows = c_pad.shape[0]
    return pl.pallas_call(
        _mod_kernel,
        grid=(depth, n // tn),
        in_specs=[pl.BlockSpec((rows, d), lambda l, j: (0, 0)),
                  pl.BlockSpec((1, d, tn), lambda l, j: (l, 0, j)),
                  pl.BlockSpec((1, 1, tn), lambda l, j: (l, 0, j))],
        out_specs=pl.BlockSpec((1, rows, tn), lambda l, j: (l, 0, j)),
        out_shape=jax.ShapeDtypeStruct((depth, rows, n), F32),
        compiler_params=_cparams(("arbitrary", "arbitrary")),
        name="adaln_mod",
    )(c_pad, w_mod, b_mod.reshape(depth, 1, n))


def _modulate_kernel(x_ref, sc_ref, sh_ref, u_ref):
    u_ref[...] = (x_ref[...] * (1.0 + sc_ref[0]) + sh_ref[0]).astype(u_ref.dtype)


def _modulate(x2, sc, sh, seq):
    m, d = x2.shape
    tm = 1024
    per_b = seq // tm
    vec = pl.BlockSpec((1, 1, d), lambda i: (i // per_b, 0, 0))
    return pl.pallas_call(
        _modulate_kernel,
        grid=(m // tm,),
        in_specs=[pl.BlockSpec((tm, d), lambda i: (i, 0)), vec, vec],
        out_specs=pl.BlockSpec((tm, d), lambda i: (i, 0)),
        out_shape=jax.ShapeDtypeStruct((m, d), BF16),
        compiler_params=_cparams(("arbitrary",)),
        name="modulate0",
    )(x2, sc, sh)


def _mm_kernel(x_ref, w_ref, o_ref):
    o_ref[...] = _dot(x_ref[...], w_ref[...]).astype(o_ref.dtype)


def _matmul(x, w, out_dtype, tm, tn, name):
    m, k = x.shape
    n = w.shape[1]
    return pl.pallas_call(
        _mm_kernel,
        grid=(n // tn, m // tm),
        in_specs=[pl.BlockSpec((tm, k), lambda j, i: (i, 0)),
                  pl.BlockSpec((k, tn), lambda j, i: (0, j))],
        out_specs=pl.BlockSpec((tm, tn), lambda j, i: (i, j)),
        out_shape=jax.ShapeDtypeStruct((m, n), out_dtype),
        compiler_params=_cparams(("arbitrary", "arbitrary")),
        name=name,
    )(x, w)


def _mlstm_kernel(qk_ref, v_ref, og_ref, gc_ref, gr_ref, cw_ref, cb_ref, bc_ref, br_ref, nw_ref,
                  tri_ref, trit_ref, y_ref, xp_ref, c_ref, n_ref, m_ref, *, L):
    dqk_all = M_HEADS * M_DQK

    @pl.when(pl.program_id(1) == 0)
    def _():
        xp_ref[0:CONV_HALO, :] = jnp.zeros((CONV_HALO, 2 * dqk_all), F32)
        c_ref[...] = jnp.zeros(c_ref.shape, F32)
        n_ref[...] = jnp.zeros(n_ref.shape, F32)
        m_ref[...] = jnp.zeros(m_ref.shape, F32)

    x = qk_ref[...]
    xp_ref[CONV_HALO:CONV_HALO + L, :] = x
    y = cb_ref[...]
    for j in range(CONV_W):
        off = CONV_HALO - (CONV_W - 1) + j
        y = y + cw_ref[j:j + 1, :] * xp_ref[off:off + L, :]
    xp_ref[0:CONV_HALO, :] = x[L - CONV_HALO:L, :]
    qk = y * _sigmoid(y)

    gc = gc_ref[...] + bc_ref[...]
    gr = gr_ref[...] + br_ref[...]
    bcum_c = _dot(tri_ref[...], jnp.concatenate(_split3(_log_sigmoid(gc)), axis=0))
    bcum_r = _dot(jnp.concatenate(_split3(_log_sigmoid(gr)), axis=1), trit_ref[...])

    ti = lax.broadcasted_iota(jnp.int32, (L, L), 0)
    si = lax.broadcasted_iota(jnp.int32, (L, L), 1)
    causal = si <= ti

    for h in range(M_HEADS):
        q = qk[:, h * M_DQK:(h + 1) * M_DQK] * (M_DQK ** -0.5)
        k = qk[:, dqk_all + h * M_DQK: dqk_all + (h + 1) * M_DQK]
        qb = q.astype(BF16)
        kb = k.astype(BF16)
        vb = v_ref[:, h * M_DV:(h + 1) * M_DV].astype(BF16)
        bc = bcum_c[:, M_HEADS + h:M_HEADS + h + 1]
        ic = gc[:, h:h + 1]
        br = bcum_r[M_HEADS + h:M_HEADS + h + 1, :]
        ir = gr[h:h + 1, :]
        c_prev = c_ref[h]
        n_prev = n_ref[h:h + 1, :]
        m_prev = m_ref[h:h + 1, 0:1]

        log_d = jnp.where(causal, bc - br + ir, -jnp.inf)
        m_inter = bc + m_prev
        m_t = jnp.maximum(m_inter, jnp.max(log_d, axis=1, keepdims=True))
        dw = jnp.exp(log_d - m_t)
        w_inter = jnp.exp(m_inter - m_t)
        s = _dot_nt(qb, kb) * dw
        num = _dot(s.astype(BF16), vb) + w_inter * _dot(qb, c_prev.astype(BF16))
        den = jnp.sum(s, axis=1, keepdims=True) + w_inter * jnp.sum(q * n_prev, axis=1, keepdims=True)
        hh = num * (1.0 / jnp.maximum(jnp.abs(den), jnp.exp(-m_t)))

        b_last = bc[L - 1:L, :]
        lwe = b_last - bc + ic
        m_new = jnp.maximum(b_last + m_prev, jnp.max(lwe, axis=0, keepdims=True))
        w_end = jnp.exp(lwe - m_new)
        decay = jnp.exp(b_last + m_prev - m_new)
        kw = k * w_end
        c_ref[h] = decay * c_prev + _dot_tn(kw.astype(BF16), vb)
        n_ref[h:h + 1, :] = decay * n_prev + jnp.sum(kw, axis=0, keepdims=True)
        m_ref[h:h + 1, :] = jnp.broadcast_to(m_new, (1, m_ref.shape[1]))

        hc = hh - jnp.mean(hh, axis=1, keepdims=True)
        hn = hc * lax.rsqrt(jnp.mean(hc * hc, axis=1, keepdims=True) + EPS)
        sl = slice(h * M_DV, (h + 1) * M_DV)
        y_ref[:, sl] = (hn * nw_ref[:, sl] * _sigmoid(og_ref[:, sl])).astype(y_ref.dtype)


def _tri_consts(L):
    tri = np.tril(np.ones((L, L), np.float32))
    tri3 = np.concatenate([tri, tri, tri], axis=1)
    trit3 = np.concatenate([tri.T, tri.T, tri.T], axis=0)
    return jnp.asarray(tri3, BF16), jnp.asarray(trit3, BF16)


def _mlstm(proj, gates, gates_t, conv_w, conv_b, b_ig, b_fg, norm_w, batch, seq):
    L = MLSTM_CHUNK
    nc = seq // L
    wq = 2 * M_HEADS * M_DQK
    tri3, trit3 = _tri_consts(L)
    bias_c = jnp.zeros((1, N_GATE), F32).at[0, :M_HEADS].set(b_ig).at[0, M_HEADS:2 * M_HEADS].set(b_fg)
    bias_r = jnp.concatenate([b_ig, b_fg]).reshape(2 * M_HEADS, 1)
    row = lambda b, c: (b * nc + c, 0)
    const = lambda b, c: (0, 0)
    return pl.pallas_call(
        functools.partial(_mlstm_kernel, L=L),
        grid=(batch, nc),
        in_specs=[pl.BlockSpec((L, wq), lambda b, c: (b * nc + c, 0)),
                  pl.BlockSpec((L, D_MLSTM), lambda b, c: (b * nc + c, 1)),
                  pl.BlockSpec((L, D_MLSTM), lambda b, c: (b * nc + c, 2)),
                  pl.BlockSpec((L, N_GATE), row),
                  pl.BlockSpec((None, 2 * M_HEADS, L), lambda b, c: (b, 0, c)),
                  pl.BlockSpec((CONV_W, wq), const),
                  pl.BlockSpec((1, wq), const),
                  pl.BlockSpec((1, N_GATE), const),
                  pl.BlockSpec((2 * M_HEADS, 1), const),
                  pl.BlockSpec((1, D_MLSTM), const),
                  pl.BlockSpec((L, 3 * L), const),
                  pl.BlockSpec((3 * L, L), const)],
        out_specs=pl.BlockSpec((L, D_MLSTM), row),
        out_shape=jax.ShapeDtypeStruct((batch * seq, D_MLSTM), BF16),
        scratch_shapes=[pltpu.VMEM((CONV_HALO + L, wq), F32),
                        pltpu.VMEM((M_HEADS, M_DQK, M_DV), F32),
                        pltpu.VMEM((8, M_DQK), F32),
                        pltpu.VMEM((8, 128), F32)],
        compiler_params=_cparams(("arbitrary", "arbitrary")),
        name="mlstm",
    )(proj, proj, proj, gates, gates_t, conv_w, conv_b.reshape(1, wq), bias_c, bias_r,
      norm_w.reshape(1, D_MLSTM), tri3, trit3)


def _hgrn_level_matrix(L):
    nlev = int(math.log2(L))
    blocks = [np.tril(np.ones((L, L), np.float32))]
    for lev in range(nlev):
        c = 1 << lev
        p = np.zeros((L, L), np.float32)
        for t in range(L):
            bd = (t // (2 * c)) * 2 * c + c - 1
            if t % (2 * c) >= c:
                p[t, bd + 1:t + 1] = 1.0
            else:
                p[t, t + 1:bd + 1] = 1.0
        blocks.append(p)
    p = np.concatenate(blocks, axis=0)
    return jnp.asarray(np.concatenate([p, p, p], axis=1), BF16)


def _hgrn_kernel(q_ref, f_ref, i_ref, g_ref, lbl_ref, nw_ref, p3_ref, y_ref, st_ref, *, L, layer):
    nlev = int(math.log2(L))
    dh_all = H_HEADS * H_DH

    @pl.when(pl.program_id(1) == 0)
    def _():
        st_ref[...] = jnp.zeros(st_ref.shape, F32)

    lg = lbl_ref[...]
    ex = jnp.exp(lg - jnp.max(lg, axis=0, keepdims=True))
    pr = ex / jnp.sum(ex, axis=0, keepdims=True)
    lb = jnp.zeros((1, dh_all), F32)
    for j in range(1, layer + 1):
        lb = lb + pr[j:j + 1, :]

    qf = q_ref[...]
    f = f_ref[...]
    qs = qf * _sigmoid(qf)
    c = jnp.log1p(-lb) + _log_sigmoid(f)
    a = jnp.log(lb)
    log_f = jnp.maximum(a, c) + jnp.log1p(jnp.exp(-jnp.abs(a - c)))
    kk = (1.0 - lb) * _sigmoid(-f)

    g_all = _dot(p3_ref[...], jnp.concatenate(_split3(log_f), axis=0))
    b = g_all[0:L]
    b_last = b[L - 1:L, :]
    qb = (qs * jnp.exp(b)).astype(BF16)
    kd = (kk * jnp.exp(b_last - b)).astype(BF16)
    dec = jnp.exp(b_last)

    rows = lax.broadcasted_iota(jnp.int32, (L, dh_all), 0)
    q_lv, k_lv = [qs.astype(BF16)], [kk.astype(BF16)]
    for lev in range(nlev):
        fac = jnp.exp(g_all[(1 + lev) * L:(2 + lev) * L])
        upper = (rows & (1 << lev)) != 0
        q_lv.append(jnp.where(upper, qs * fac, 0.0).astype(BF16))
        k_lv.append(jnp.where(upper, 0.0, kk * fac).astype(BF16))

    ti = lax.broadcasted_iota(jnp.int32, (L, L), 0)
    si = lax.broadcasted_iota(jnp.int32, (L, L), 1)
    tx = ti ^ si

    for h in range(H_HEADS):
        sl = slice(h * H_DH, (h + 1) * H_DH)
        vb = i_ref[:, sl].astype(BF16)
        att = jnp.where(tx == 0, _dot_nt(q_lv[0][:, sl], k_lv[0][:, sl]), 0.0)
        for lev in range(nlev):
            term = _dot_nt(q_lv[1 + lev][:, sl], k_lv[1 + lev][:, sl])
            if lev < nlev - 1:
                term = jnp.where(tx < (2 << lev), term, 0.0)
            att = att + term
        st = st_ref[h]
        o = _dot(att.astype(BF16), vb) + _dot_nt(qb[:, sl], st.astype(BF16))
        st_ref[h] = st * dec[:, sl] + _dot_tn(vb, kd[:, sl])
        on = o * lax.rsqrt(jnp.mean(o * o, axis=1, keepdims=True) + EPS)
        g = g_ref[:, sl]
        y_ref[:, sl] = (on * nw_ref[:, sl] * (g * _sigmoid(g))).astype(y_ref.dtype)


def _hgrn(proj, lb_logits, norm_w, layer, batch, seq):
    L = HGRN_CHUNK
    nc = seq // L
    p3 = _hgrn_level_matrix(L)
    const = lambda b, c: (0, 0)
    col = lambda j: pl.BlockSpec((L, D_HGRN), lambda b, c: (b * nc + c, j))
    return pl.pallas_call(
        functools.partial(_hgrn_kernel, L=L, layer=layer),
        grid=(batch, nc),
        in_specs=[col(3), col(4), col(5), col(6),
                  pl.BlockSpec(lb_logits.shape, const),
                  pl.BlockSpec((1, D_HGRN), const),
                  pl.BlockSpec(p3.shape, const)],
        out_specs=pl.BlockSpec((L, D_HGRN), lambda b, c: (b * nc + c, 0)),
        out_shape=jax.ShapeDtypeStruct((batch * seq, D_HGRN), BF16),
        scratch_shapes=[pltpu.VMEM((H_HEADS, H_DH, H_DH), F32)],
        compiler_params=_cparams(("arbitrary", "arbitrary")),
        name="hgrn2",
    )(proj, proj, proj, proj, lb_logits, norm_w.reshape(1, D_HGRN), p3)


def _residual_ln(y, x, gate, ln_g, ln_b, sc, sh, xo_ref, uo_ref):
    z = ALPHA * x + (1.0 + gate) * y
    mu = jnp.mean(z, axis=1, keepdims=True)
    zc = z - mu
    xn = zc * lax.rsqrt(jnp.mean(zc * zc, axis=1, keepdims=True) + EPS) * ln_g + ln_b
    xo_ref[...] = xn
    uo_ref[...] = (xn * (1.0 + sc) + sh).astype(uo_ref.dtype)


def _outproj_kernel(ym_ref, yh_ref, w_ref, x_ref, gate_ref, lng_ref, lnb_ref, sc_ref, sh_ref, xo_ref, uo_ref):
    y = _dot(ym_ref[...], w_ref[0:D_MLSTM, :]) + _dot(yh_ref[...], w_ref[D_MLSTM:D_MLSTM + D_HGRN, :])
    _residual_ln(y, x_ref[...], gate_ref[0], lng_ref[...], lnb_ref[...], sc_ref[0], sh_ref[0], xo_ref, uo_ref)


def _outproj_ln(ym, yh, w, x2, gate, ln_g, ln_b, sc, sh, seq):
    m, d = x2.shape
    tm = 512
    per_b = seq // tm
    vec = pl.BlockSpec((1, 1, d), lambda i: (i // per_b, 0, 0))
    par = pl.BlockSpec((1, d), lambda i: (0, 0))
    rowblk = lambda width: pl.BlockSpec((tm, width), lambda i: (i, 0))
    return pl.pallas_call(
        _outproj_kernel,
        grid=(m // tm,),
        in_specs=[rowblk(D_MLSTM), rowblk(D_HGRN), pl.BlockSpec(w.shape, lambda i: (0, 0)), rowblk(d),
                  vec, par, par, vec, vec],
        out_specs=[rowblk(d), rowblk(d)],
        out_shape=[jax.ShapeDtypeStruct((m, d), F32), jax.ShapeDtypeStruct((m, d), BF16)],
        compiler_params=_cparams(("arbitrary",)),
        name="outproj_ln",
    )(ym, yh, w, x2, gate, ln_g.reshape(1, d), ln_b.reshape(1, d), sc, sh)


def _ffn_up_kernel(u_ref, wg_ref, wu_ref, h_ref):
    u = u_ref[...]
    g = _dot(u, wg_ref[...])
    h_ref[...] = (g * _sigmoid(g) * _dot(u, wu_ref[...])).astype(h_ref.dtype)


def _ffn_up(u, wg, wu):
    m, k = u.shape
    n = wg.shape[1]
    tm, tn = 1024, 512
    wspec = pl.BlockSpec((k, tn), lambda j, i: (0, j))
    return pl.pallas_call(
        _ffn_up_kernel,
        grid=(n // tn, m // tm),
        in_specs=[pl.BlockSpec((tm, k), lambda j, i: (i, 0)), wspec, wspec],
        out_specs=pl.BlockSpec((tm, tn), lambda j, i: (i, j)),
        out_shape=jax.ShapeDtypeStruct((m, n), BF16),
        compiler_params=_cparams(("arbitrary", "arbitrary")),
        name="ffn_up",
    )(u, wg, wu)


def _down_kernel(h_ref, w_ref, x_ref, gate_ref, lng_ref, lnb_ref, sc_ref, sh_ref, xo_ref, uo_ref, acc_ref):
    kstep = pl.program_id(1)

    @pl.when(kstep == 0)
    def _():
        acc_ref[...] = jnp.zeros(acc_ref.shape, F32)

    acc_ref[...] += _dot(h_ref[...], w_ref[...])

    @pl.when(kstep == pl.num_programs(1) - 1)
    def _():
        _residual_ln(acc_ref[...], x_ref[...], gate_ref[0], lng_ref[...], lnb_ref[...], sc_ref[0], sh_ref[0],
                     xo_ref, uo_ref)


def _down_ln(hid, w, x2, gate, ln_g, ln_b, sc, sh, seq):
    m, d = x2.shape
    kdim = hid.shape[1]
    tm, tk = 512, 1408
    per_b = seq // tm
    vec = pl.BlockSpec((1, 1, d), lambda i, k: (i // per_b, 0, 0))
    par = pl.BlockSpec((1, d), lambda i, k: (0, 0))
    rowblk = pl.BlockSpec((tm, d), lambda i, k: (i, 0))
    return pl.pallas_call(
        _down_kernel,
        grid=(m // tm, kdim // tk),
        in_specs=[pl.BlockSpec((tm, tk), lambda i, k: (i, k)), pl.BlockSpec((tk, d), lambda i, k: (k, 0)),
                  rowblk, vec, par, par, vec, vec],
        out_specs=[rowblk, rowblk],
        out_shape=[jax.ShapeDtypeStruct((m, d), F32), jax.ShapeDtypeStruct((m, d), BF16)],
        scratch_shapes=[pltpu.VMEM((tm, d), F32)],
        compiler_params=_cparams(("arbitrary", "arbitrary")),
        name="ffn_down_ln",
    )(hid, w, x2, gate, ln_g.reshape(1, d), ln_b.reshape(1, d), sc, sh)


def _split_w_in(w_in_l):
    g0 = 3 * 1024
    g1 = g0 + 2 * M_HEADS
    big = jnp.concatenate([w_in_l[:, :g0], w_in_l[:, g1:]], axis=1).astype(BF16)
    gate = jnp.pad(w_in_l[:, g0:g1], ((0, 0), (0, N_GATE - 2 * M_HEADS))).astype(BF16)
    return big, gate


def kernel(x, c, w_mod, b_mod, w_in, conv_w, conv_b, b_igate, b_fgate, mlstm_norm_w, hgrn_norm_w, lb_logits, w_out, ln1_g, ln1_b, w_gate, w_up, w_down, ln2_g, ln2_b):
    batch, seq, d = x.shape
    depth = w_mod.shape[0]
    m = batch * seq
    x2 = x.reshape(m, d)

    c_pad = jnp.pad(c, ((0, 8 - batch), (0, 0)))
    mod = _mod_all(c_pad, w_mod, b_mod)[:, :batch]
    mod = mod.reshape(depth, batch, N_MOD, 1, d)
    mvec = lambda l, k: mod[l, :, k]

    u = _modulate(x2, mvec(0, 1), mvec(0, 0), seq)
    for l in range(depth):
        w_big, w_g = _split_w_in(w_in[l])
        proj = _matmul(u, w_big, F32, 1024, 1024, "in_proj")
        gates = _matmul(u, w_g, F32, 1024, N_GATE, "in_proj_gates")
        gates_t = jnp.swapaxes(gates[:, :2 * M_HEADS].reshape(batch, seq, 2 * M_HEADS), 1, 2)
        ym = _mlstm(proj, gates, gates_t, conv_w[l], conv_b[l], b_igate[l], b_fgate[l], mlstm_norm_w[l],
                    batch, seq)
        yh = _hgrn(proj, lb_logits, hgrn_norm_w[l], l, batch, seq)
        x2, u = _outproj_ln(ym, yh, w_out[l].astype(BF16), x2, mvec(l, 2), ln1_g[l], ln1_b[l],
                            mvec(l, 4), mvec(l, 3), seq)
        hid = _ffn_up(u, w_gate[l].astype(BF16), w_up[l].astype(BF16))
        nl = min(l + 1, depth - 1)
        x2, u = _down_ln(hid, w_down[l].astype(BF16), x2, mvec(l, 5), ln2_g[l], ln2_b[l],
                         mvec(nl, 1), mvec(nl, 0), seq)
    return x2.reshape(batch, seq, d)
```

```python
import functools
import math

import numpy as np
import jax
import jax.numpy as jnp
from jax import lax
from jax.experimental import pallas as pl
from jax.experimental.pallas import tpu as pltpu

F32 = jnp.float32
BF16 = jnp.bfloat16

D_MODEL = 2048
DEPTH = 4
D_MLSTM = 1024
D_HGRN = 1024
M_HEADS = 4
M_DV = 256
M_DQK = 128
H_HEADS = 8
H_DH = 128
CONV_W = 4
D_FF = 5632
N_MOD = 6
EPS = 1e-5
ALPHA = (2 * DEPTH) ** 0.25
N_GATE = 128
SUBLANES = 8
CONV_HALO = SUBLANES
LOG2E = 1.4426950408889634

VMEM_LIMIT = 56 * 1024 * 1024

MLSTM_CHUNK = 256
HGRN_CHUNK = 128


def _cparams(sem):
    return pltpu.CompilerParams(dimension_semantics=sem, vmem_limit_bytes=VMEM_LIMIT)


def _sigmoid(x):
    return 1.0 / (1.0 + jnp.exp(-x))


def _log_sigmoid(x):
    return jnp.minimum(x, 0.0) - jnp.log(1.0 + jnp.exp(-jnp.abs(x)))


def _split3(x):
    hi = x.astype(BF16)
    r1 = x - hi.astype(F32)
    mid = r1.astype(BF16)
    lo = (r1 - mid.astype(F32)).astype(BF16)
    return hi, mid, lo


def _dot(a, b):
    return jnp.dot(a, b, preferred_element_type=F32)


def _dot_nt(a, b):
    return lax.dot_general(a, b, (((1,), (1,)), ((), ())), preferred_element_type=F32)


def _dot_tn(a, b):
    return lax.dot_general(a, b, (((0,), (0,)), ((), ())), preferred_element_type=F32)


def _mod_kernel(c_ref, w_ref, b_ref, o_ref):
    c = c_ref[...]
    ca = (c * _sigmoid(c)).astype(BF16)
    o_ref[0] = _dot(ca, w_ref[0].astype(BF16)) + b_ref[0]


def _mod_all(c_pad, w_mod, b_mod):
    depth, d, n = w_mod.shape
    tn = 1024
    rows = c_pad.shape[0]
    return pl.pallas_call(
        _mod_kernel,
        grid=(depth, n // tn),
        in_specs=[pl.BlockSpec((rows, d), lambda l, j: (0, 0)),
                  pl.BlockSpec((1, d, tn), lambda l, j: (l, 0, j)),
                  pl.BlockSpec((1, 1, tn), lambda l, j: (l, 0, j))],
        out_specs=pl.BlockSpec((1, rows, tn), lambda l, j: (l, 0, j)),
        out_shape=jax.ShapeDtypeStruct((depth, rows, n), F32),
        compiler_params=_cparams(("arbitrary", "arbitrary")),
        name="adaln_mod",
    )(c_pad, w_mod, b_mod.reshape(depth, 1, n))


def _modulate_kernel(x_ref, sc_ref, sh_ref, u_ref):
    u_ref[...] = (x_ref[...] * (1.0 + sc_ref[0]) + sh_ref[0]).astype(u_ref.dtype)


def _modulate(x2, sc, sh, seq):
    m, d = x2.shape
    tm = 1024
    per_b = seq // tm
    vec = pl.BlockSpec((1, 1, d), lambda i: (i // per_b, 0, 0))
    return pl.pallas_call(
        _modulate_kernel,
        grid=(m // tm,),
        in_specs=[pl.BlockSpec((tm, d), lambda i: (i, 0)), vec, vec],
        out_specs=pl.BlockSpec((tm, d), lambda i: (i, 0)),
        out_shape=jax.ShapeDtypeStruct((m, d), BF16),
        compiler_params=_cparams(("arbitrary",)),
        name="modulate0",
    )(x2, sc, sh)


def _mm_kernel(x_ref, w_ref, o_ref):
    o_ref[...] = _dot(x_ref[...], w_ref[...]).astype(o_ref.dtype)


def _matmul(x, w, out_dtype, tm, tn, name):
    m, k = x.shape
    n = w.shape[1]
    return pl.pallas_call(
        _mm_kernel,
        grid=(n // tn, m // tm),
        in_specs=[pl.BlockSpec((tm, k), lambda j, i: (i, 0)),
                  pl.BlockSpec((k, tn), lambda j, i: (0, j))],
        out_specs=pl.BlockSpec((tm, tn), lambda j, i: (i, j)),
        out_shape=jax.ShapeDtypeStruct((m, n), out_dtype),
        compiler_params=_cparams(("arbitrary", "arbitrary")),
        name=name,
    )(x, w)


def _mlstm_kernel(qk_ref, v_ref, og_ref, gc_ref, gr_ref, cw_ref, cb_ref, bc_ref, br_ref, nw_ref,
                  tri_ref, trit_ref, y_ref, xp_ref, c_ref, n_ref, m_ref, *, L):
    dqk_all = M_HEADS * M_DQK

    @pl.when(pl.program_id(1) == 0)
    def _():
        xp_ref[0:CONV_HALO, :] = jnp.zeros((CONV_HALO, 2 * dqk_all), F32)
        c_ref[...] = jnp.zeros(c_ref.shape, F32)
        n_ref[...] = jnp.zeros(n_ref.shape, F32)
        m_ref[...] = jnp.zeros(m_ref.shape, F32)

    x = qk_ref[...]
    xp_ref[CONV_HALO:CONV_HALO + L, :] = x
    y = cb_ref[...]
    for j in range(CONV_W):
        off = CONV_HALO - (CONV_W - 1) + j
        y = y + cw_ref[j:j + 1, :] * xp_ref[off:off + L, :]
    xp_ref[0:CONV_HALO, :] = x[L - CONV_HALO:L, :]
    qk = y * _sigmoid(y)

    gc = gc_ref[...] + bc_ref[...]
    gr = gr_ref[...] + br_ref[...]
    bcum_c = _dot(tri_ref[...], jnp.concatenate(_split3(_log_sigmoid(gc) * LOG2E), axis=0))
    bcum_r = _dot(jnp.concatenate(_split3(_log_sigmoid(gr) * LOG2E), axis=1), trit_ref[...])
    gc = gc * LOG2E
    gr = gr * LOG2E

    ti = lax.broadcasted_iota(jnp.int32, (L, L), 0)
    si = lax.broadcasted_iota(jnp.int32, (L, L), 1)
    causal = si <= ti

    for h in range(M_HEADS):
        q = qk[:, h * M_DQK:(h + 1) * M_DQK] * (M_DQK ** -0.5)
        k = qk[:, dqk_all + h * M_DQK: dqk_all + (h + 1) * M_DQK]
        qb = q.astype(BF16)
        kb = k.astype(BF16)
        vb = v_ref[:, h * M_DV:(h + 1) * M_DV].astype(BF16)
        bc = bcum_c[:, M_HEADS + h:M_HEADS + h + 1]
        ic = gc[:, h:h + 1]
        br = bcum_r[M_HEADS + h:M_HEADS + h + 1, :]
        ir = gr[h:h + 1, :]
        c_prev = c_ref[h]
        n_prev = n_ref[h:h + 1, :]
        m_prev = m_ref[h:h + 1, 0:1]

        log_d = jnp.where(causal, bc - br + ir, -jnp.inf)
        m_inter = bc + m_prev
        m_t = jnp.maximum(m_inter, jnp.max(log_d, axis=1, keepdims=True))
        dw = jnp.exp2(log_d - m_t)
        w_inter = jnp.exp2(m_inter - m_t)
        s = _dot_nt(qb, kb) * dw
        num = _dot(s.astype(BF16), vb) + w_inter * _dot(qb, c_prev.astype(BF16))
        den = jnp.sum(s, axis=1, keepdims=True) + w_inter * jnp.sum(q * n_prev, axis=1, keepdims=True)
        hh = num * (1.0 / jnp.maximum(jnp.abs(den), jnp.exp2(-m_t)))

        b_last = bc[L - 1:L, :]
        lwe = b_last - bc + ic
        m_new = jnp.maximum(b_last + m_prev, jnp.max(lwe, axis=0, keepdims=True))
        w_end = jnp.exp2(lwe - m_new)
        decay = jnp.exp2(b_last + m_prev - m_new)
        kw = k * w_end
        c_ref[h] = decay * c_prev + _dot_tn(kw.astype(BF16), vb)
        n_ref[h:h + 1, :] = decay * n_prev + jnp.sum(kw, axis=0, keepdims=True)
        m_ref[h:h + 1, :] = jnp.broadcast_to(m_new, (1, m_ref.shape[1]))

        hc = hh - jnp.mean(hh, axis=1, keepdims=True)
        hn = hc * lax.rsqrt(jnp.mean(hc * hc, axis=1, keepdims=True) + EPS)
        sl = slice(h * M_DV, (h + 1) * M_DV)
        y_ref[:, sl] = (hn * nw_ref[:, sl] * _sigmoid(og_ref[:, sl])).astype(y_ref.dtype)


def _tri_consts(L):
    tri = np.tril(np.ones((L, L), np.float32))
    tri3 = np.concatenate([tri, tri, tri], axis=1)
    trit3 = np.concatenate([tri.T, tri.T, tri.T], axis=0)
    return jnp.asarray(tri3, BF16), jnp.asarray(trit3, BF16)


def _mlstm(proj, gates, gates_t, conv_w, conv_b, b_ig, b_fg, norm_w, batch, seq):
    L = MLSTM_CHUNK
    nc = seq // L
    wq = 2 * M_HEADS * M_DQK
    tri3, trit3 = _tri_consts(L)
    bias_c = jnp.zeros((1, N_GATE), F32).at[0, :M_HEADS].set(b_ig).at[0, M_HEADS:2 * M_HEADS].set(b_fg)
    bias_r = jnp.concatenate([b_ig, b_fg]).reshape(2 * M_HEADS, 1)
    row = lambda b, c: (b * nc + c, 0)
    const = lambda b, c: (0, 0)
    return pl.pallas_call(
        functools.partial(_mlstm_kernel, L=L),
        grid=(batch, nc),
        in_specs=[pl.BlockSpec((L, wq), lambda b, c: (b * nc + c, 0)),
                  pl.BlockSpec((L, D_MLSTM), lambda b, c: (b * nc + c, 1)),
                  pl.BlockSpec((L, D_MLSTM), lambda b, c: (b * nc + c, 2)),
                  pl.BlockSpec((L, N_GATE), row),
                  pl.BlockSpec((None, 2 * M_HEADS, L), lambda b, c: (b, 0, c)),
                  pl.BlockSpec((CONV_W, wq), const),
                  pl.BlockSpec((1, wq), const),
                  pl.BlockSpec((1, N_GATE), const),
                  pl.BlockSpec((2 * M_HEADS, 1), const),
                  pl.BlockSpec((1, D_MLSTM), const),
                  pl.BlockSpec((L, 3 * L), const),
                  pl.BlockSpec((3 * L, L), const)],
        out_specs=pl.BlockSpec((L, D_MLSTM), row),
        out_shape=jax.ShapeDtypeStruct((batch * seq, D_MLSTM), BF16),
        scratch_shapes=[pltpu.VMEM((CONV_HALO + L, wq), F32),
                        pltpu.VMEM((M_HEADS, M_DQK, M_DV), F32),
                        pltpu.VMEM((8, M_DQK), F32),
                        pltpu.VMEM((8, 128), F32)],
        compiler_params=_cparams(("arbitrary", "arbitrary")),
        name="mlstm",
    )(proj, proj, proj, gates, gates_t, conv_w, conv_b.reshape(1, wq), bias_c, bias_r,
      norm_w.reshape(1, D_MLSTM), tri3, trit3)


def _hgrn_kernel(q_ref, f_ref, i_ref, g_ref, lbl_ref, nw_ref, tri_ref, y_ref, st_ref, b_ref, *, L, layer):
    dh_all = H_HEADS * H_DH
    zero = lambda n: jnp.zeros((n, dh_all), F32)

    @pl.when(pl.program_id(1) == 0)
    def _():
        st_ref[...] = jnp.zeros(st_ref.shape, F32)

    qf = q_ref[...]
    f = f_ref[...]
    qs = qf * _sigmoid(qf)
    e = jnp.exp(-jnp.abs(f))
    d = 1.0 + e
    log_f = jnp.minimum(f, 0.0) - jnp.log(d)
    kk = jnp.where(f >= 0.0, e, 1.0) / d
    if layer > 0:
        lg = lbl_ref[...]
        ex = jnp.exp(lg - jnp.max(lg, axis=0, keepdims=True))
        pr = ex / jnp.sum(ex, axis=0, keepdims=True)
        lb = pr[1:2, :]
        for j in range(2, layer + 1):
            lb = lb + pr[j:j + 1, :]
        c = jnp.log1p(-lb) + log_f
        a = jnp.log(lb)
        log_f = jnp.maximum(a, c) + jnp.log(1.0 + jnp.exp(-jnp.abs(a - c)))
        kk = (1.0 - lb) * kk

    lf2 = log_f * LOG2E
    b = _dot(tri_ref[...], jnp.concatenate(_split3(lf2), axis=0))
    b_ref[...] = b
    b_last = b[L - 1:L, :]
    qb = (qs * jnp.exp2(b)).astype(BF16)
    kd = (kk * jnp.exp2(b_last - b)).astype(BF16)
    dec = jnp.exp2(b_last)

    sub = lax.broadcasted_iota(jnp.int32, (SUBLANES, dh_all), 0)
    groups = range(L // SUBLANES)
    grp = lambda v, g: v[g * SUBLANES:(g + 1) * SUBLANES]
    odd = (sub & 1) != 0
    f2 = jnp.exp2(lf2)
    q_lv = [qs.astype(BF16),
            jnp.concatenate([jnp.where(odd, grp(qs, g) * grp(f2, g), 0.0) for g in groups], axis=0).astype(BF16)]
    k_lv = [kk.astype(BF16),
            jnp.concatenate([jnp.where(odd, 0.0, grp(kk, g)) for g in groups], axis=0).astype(BF16)]
    half = 2
    while half < L:
        qp, kp = [], []
        if half < SUBLANES:
            upper = (sub & half) != 0
            for g in groups:
                r = None
                for mblk in range(SUBLANES // (2 * half)):
                    cand = jnp.broadcast_to(b_ref[pl.ds(g * SUBLANES + mblk * 2 * half + half - 1, 1), :],
                                            (SUBLANES, dh_all))
                    r = cand if r is None else jnp.where(sub >= mblk * 2 * half, cand, r)
                fac = jnp.exp2(-jnp.abs(grp(b, g) - r))
                qp.append(jnp.where(upper, grp(qs, g) * fac, 0.0))
                kp.append(jnp.where(upper, 0.0, grp(kk, g) * fac))
        else:
            for j in range(L // (2 * half)):
                lo = slice(j * 2 * half, j * 2 * half + half)
                up = slice(j * 2 * half + half, (j + 1) * 2 * half)
                r = b_ref[pl.ds(j * 2 * half + half - 1, 1), :]
                kp += [kk[lo] * jnp.exp2(r - b[lo]), zero(half)]
                qp += [zero(half), qs[up] * jnp.exp2(b[up] - r)]
        q_lv.append(jnp.concatenate(qp, axis=0).astype(BF16))
        k_lv.append(jnp.concatenate(kp, axis=0).astype(BF16))
        half *= 2

    ti = lax.broadcasted_iota(jnp.int32, (L, L), 0)
    si = lax.broadcasted_iota(jnp.int32, (L, L), 1)
    tx = ti ^ si

    atts = []
    for h in range(H_HEADS):
        sl = slice(h * H_DH, (h + 1) * H_DH)
        att = _dot_nt(q_lv[-1][:, sl], k_lv[-1][:, sl])
        for lev in range(len(q_lv) - 2, -1, -1):
            att = jnp.where(tx < (1 << lev), _dot_nt(q_lv[lev][:, sl], k_lv[lev][:, sl]), att)
        atts.append(att.astype(BF16))

    for h in range(H_HEADS):
        sl = slice(h * H_DH, (h + 1) * H_DH)
        vb = i_ref[:, sl].astype(BF16)
        st = st_ref[h]
        o = _dot(atts[h], vb) + _dot_nt(qb[:, sl], st.astype(BF16))
        st_ref[h] = st * dec[:, sl] + _dot_tn(vb, kd[:, sl])
        on = o * lax.rsqrt(jnp.mean(o * o, axis=1, keepdims=True) + EPS)
        g = g_ref[:, sl]
        y_ref[:, sl] = (on * nw_ref[:, sl] * (g * _sigmoid(g))).astype(y_ref.dtype)


def _hgrn(proj, lb_logits, norm_w, layer, batch, seq):
    L = HGRN_CHUNK
    nc = seq // L
    tri3, _ = _tri_consts(L)
    const = lambda b, c: (0, 0)
    col = lambda j: pl.BlockSpec((L, D_HGRN), lambda b, c: (b * nc + c, j))
    return pl.pallas_call(
        functools.partial(_hgrn_kernel, L=L, layer=layer),
        grid=(batch, nc),
        in_specs=[col(3), col(4), col(5), col(6),
                  pl.BlockSpec(lb_logits.shape, const),
                  pl.BlockSpec((1, D_HGRN), const),
                  pl.BlockSpec(tri3.shape, const)],
        out_specs=pl.BlockSpec((L, D_HGRN), lambda b, c: (b * nc + c, 0)),
        out_shape=jax.ShapeDtypeStruct((batch * seq, D_HGRN), BF16),
        scratch_shapes=[pltpu.VMEM((H_HEADS, H_DH, H_DH), F32), pltpu.VMEM((L, D_HGRN), F32)],
        compiler_params=_cparams(("arbitrary", "arbitrary")),
        name="hgrn2",
    )(proj, proj, proj, proj, lb_logits, norm_w.reshape(1, D_HGRN), tri3)


def _ln_epilogue(y, rows, x_ref, gate_ref, lng_ref, lnb_ref, sc_ref, sh_ref, xo_ref, uo_ref):
    z = ALPHA * x_ref[rows, :] + (1.0 + gate_ref[0]) * y
    zc = z - jnp.mean(z, axis=1, keepdims=True)
    xn = zc * lax.rsqrt(jnp.mean(zc * zc, axis=1, keepdims=True) + EPS) * lng_ref[...] + lnb_ref[...]
    xo_ref[rows, :] = xn
    if uo_ref is not None:
        uo_ref[rows, :] = (xn * (1.0 + sc_ref[0]) + sh_ref[0]).astype(uo_ref.dtype)


def _outproj_kernel(ym_ref, yh_ref, w_ref, x_ref, gate_ref, lng_ref, lnb_ref, sc_ref, sh_ref, xo_ref, uo_ref,
                    wb_ref, *, sub):
    @pl.when(pl.program_id(0) == 0)
    def _():
        wb_ref[...] = w_ref[...].astype(BF16)

    for s in range(ym_ref.shape[0] // sub):
        rows = pl.ds(s * sub, sub)
        y = (_dot(ym_ref[rows, :], wb_ref[0:D_MLSTM, :]) +
             _dot(yh_ref[rows, :], wb_ref[D_MLSTM:D_MLSTM + D_HGRN, :]))
        _ln_epilogue(y, rows, x_ref, gate_ref, lng_ref, lnb_ref, sc_ref, sh_ref, xo_ref, uo_ref)


def _outproj_ln(ym, yh, w_all, layer, x2, gate, ln_g, ln_b, sc, sh, seq):
    m, d = x2.shape
    wshape = w_all.shape[1:]
    tm, sub = 512, 256
    per_b = seq // tm
    vec = pl.BlockSpec((1, 1, d), lambda i: (i // per_b, 0, 0))
    par = pl.BlockSpec((1, d), lambda i: (0, 0))
    rowblk = lambda width: pl.BlockSpec((tm, width), lambda i: (i, 0))
    return pl.pallas_call(
        functools.partial(_outproj_kernel, sub=sub),
        grid=(m // tm,),
        in_specs=[rowblk(D_MLSTM), rowblk(D_HGRN),
                  pl.BlockSpec((None,) + wshape, lambda i: (layer, 0, 0), pipeline_mode=pl.Buffered(1)),
                  rowblk(d), vec, par, par, vec, vec],
        out_specs=[rowblk(d), rowblk(d)],
        out_shape=[jax.ShapeDtypeStruct((m, d), F32), jax.ShapeDtypeStruct((m, d), BF16)],
        scratch_shapes=[pltpu.VMEM(wshape, BF16)],
        compiler_params=_cparams(("arbitrary",)),
        name="outproj_ln",
    )(ym, yh, w_all, x2, gate, ln_g.reshape(1, d), ln_b.reshape(1, d), sc, sh)


def _ffn_up_kernel(u_ref, wg_ref, wu_ref, h_ref, wgb_ref, wub_ref):
    @pl.when(pl.program_id(1) == 0)
    def _():
        wgb_ref[...] = wg_ref[...].astype(BF16)
        wub_ref[...] = wu_ref[...].astype(BF16)

    u = u_ref[...]
    g = _dot(u, wgb_ref[...])
    h_ref[...] = (g * _sigmoid(g) * _dot(u, wub_ref[...])).astype(h_ref.dtype)


def _ffn_up(u, wg_all, wu_all, layer):
    m, k = u.shape
    n = wg_all.shape[2]
    tm, tn = 1024, 512
    wspec = pl.BlockSpec((None, k, tn), lambda j, i: (layer, 0, j))
    return pl.pallas_call(
        _ffn_up_kernel,
        grid=(n // tn, m // tm),
        in_specs=[pl.BlockSpec((tm, k), lambda j, i: (i, 0)), wspec, wspec],
        out_specs=pl.BlockSpec((tm, tn), lambda j, i: (i, j)),
        out_shape=jax.ShapeDtypeStruct((m, n), BF16),
        scratch_shapes=[pltpu.VMEM((k, tn), BF16), pltpu.VMEM((k, tn), BF16)],
        compiler_params=_cparams(("arbitrary", "arbitrary")),
        name="ffn_up",
    )(u, wg_all, wu_all)


def _down_kernel(h_ref, w_ref, x_ref, gate_ref, lng_ref, lnb_ref, sc_ref, sh_ref, *out_refs, sub, emit_u):
    xo_ref = out_refs[0]
    uo_ref = out_refs[1] if emit_u else None
    for s in range(h_ref.shape[0] // sub):
        rows = pl.ds(s * sub, sub)
        y = _dot(h_ref[rows, :], w_ref[...])
        _ln_epilogue(y, rows, x_ref, gate_ref, lng_ref, lnb_ref, sc_ref, sh_ref, xo_ref, uo_ref)


def _down_ln(hid, w, x2, gate, ln_g, ln_b, sc, sh, seq, emit_u):
    m, d = x2.shape
    kdim = hid.shape[1]
    tm, sub = 256, 128
    per_b = seq // tm
    vec = pl.BlockSpec((1, 1, d), lambda i: (i // per_b, 0, 0))
    par = pl.BlockSpec((1, d), lambda i: (0, 0))
    rowblk = pl.BlockSpec((tm, d), lambda i: (i, 0))
    n_out = 2 if emit_u else 1
    return pl.pallas_call(
        functools.partial(_down_kernel, sub=sub, emit_u=emit_u),
        grid=(m // tm,),
        in_specs=[pl.BlockSpec((tm, kdim), lambda i: (i, 0)),
                  pl.BlockSpec(w.shape, lambda i: (0, 0), pipeline_mode=pl.Buffered(1)),
                  rowblk, vec, par, par, vec, vec],
        out_specs=[rowblk, rowblk][:n_out],
        out_shape=[jax.ShapeDtypeStruct((m, d), F32), jax.ShapeDtypeStruct((m, d), BF16)][:n_out],
        compiler_params=_cparams(("arbitrary",)),
        name="ffn_down_ln",
    )(hid, w, x2, gate, ln_g.reshape(1, d), ln_b.reshape(1, d), sc, sh)


def _split_w_in(w_in_l):
    g0 = 3 * 1024
    g1 = g0 + 2 * M_HEADS
    big = jnp.concatenate([w_in_l[:, :g0], w_in_l[:, g1:]], axis=1).astype(BF16)
    gate = jnp.pad(w_in_l[:, g0:g1], ((0, 0), (0, N_GATE - 2 * M_HEADS))).astype(BF16)
    return big, gate


def kernel(x, c, w_mod, b_mod, w_in, conv_w, conv_b, b_igate, b_fgate, mlstm_norm_w, hgrn_norm_w, lb_logits, w_out, ln1_g, ln1_b, w_gate, w_up, w_down, ln2_g, ln2_b):
    batch, seq, d = x.shape
    depth = w_mod.shape[0]
    m = batch * seq
    x2 = x.reshape(m, d)

    c_pad = jnp.pad(c, ((0, 8 - batch), (0, 0)))
    mod = _mod_all(c_pad, w_mod, b_mod)[:, :batch]
    mod = mod.reshape(depth, batch, N_MOD, 1, d)
    mvec = lambda l, k: mod[l, :, k]

    u = _modulate(x2, mvec(0, 1), mvec(0, 0), seq)
    for l in range(depth):
        w_big, w_g = _split_w_in(w_in[l])
        proj = _matmul(u, w_big, F32, 1024, 1024, "in_proj")
        gates = _matmul(u, w_g, F32, 1024, N_GATE, "in_proj_gates")
        gates_t = jnp.swapaxes(gates[:, :2 * M_HEADS].reshape(batch, seq, 2 * M_HEADS), 1, 2)
        ym = _mlstm(proj, gates, gates_t, conv_w[l], conv_b[l], b_igate[l], b_fgate[l], mlstm_norm_w[l],
                    batch, seq)
        yh = _hgrn(proj, lb_logits, hgrn_norm_w[l], l, batch, seq)
        x2, u = _outproj_ln(ym, yh, w_out, l, x2, mvec(l, 2), ln1_g[l], ln1_b[l], mvec(l, 4), mvec(l, 3), seq)
        hid = _ffn_up(u, w_gate, w_up, l)
        last = l == depth - 1
        nl = l if last else l + 1
        outs = _down_ln(hid, w_down[l].astype(BF16), x2, mvec(l, 5), ln2_g[l], ln2_b[l],
                        mvec(nl, 1), mvec(nl, 0), seq, not last)
        x2 = outs[0]
        u = None if last else outs[1]
    return x2.reshape(batch, seq, d)
```

```python
import functools
import math

import numpy as np
import jax
import jax.numpy as jnp
from jax import lax
from jax.experimental import pallas as pl
from jax.experimental.pallas import tpu as pltpu

F32 = jnp.float32
BF16 = jnp.bfloat16

D_MODEL = 2048
DEPTH = 4
D_MLSTM = 1024
D_HGRN = 1024
M_HEADS = 4
M_DV = 256
M_DQK = 128
H_HEADS = 8
H_DH = 128
CONV_W = 4
D_FF = 5632
N_MOD = 6
EPS = 1e-5
ALPHA = (2 * DEPTH) ** 0.25
N_GATE = 128
SUBLANES = 8
BF16_ROWS = 16
CONV_HALO = SUBLANES
LOG2E = 1.4426950408889634

VMEM_LIMIT = 56 * 1024 * 1024

MLSTM_CHUNK = 256
HGRN_CHUNK = 128


def _cparams(sem):
    return pltpu.CompilerParams(dimension_semantics=sem, vmem_limit_bytes=VMEM_LIMIT)


def _sigmoid(x):
    return 1.0 / (1.0 + jnp.exp(-x))


def _log_sigmoid(x):
    return jnp.minimum(x, 0.0) - jnp.log(1.0 + jnp.exp(-jnp.abs(x)))


def _split3(x):
    hi = x.astype(BF16)
    r1 = x - hi.astype(F32)
    mid = r1.astype(BF16)
    lo = (r1 - mid.astype(F32)).astype(BF16)
    return hi, mid, lo


def _dot(a, b):
    return jnp.dot(a, b, preferred_element_type=F32)


def _dot_nt(a, b):
    return lax.dot_general(a, b, (((1,), (1,)), ((), ())), preferred_element_type=F32)


def _dot_tn(a, b):
    return lax.dot_general(a, b, (((0,), (0,)), ((), ())), preferred_element_type=F32)


def _mod_kernel(c_ref, w_ref, b_ref, o_ref):
    c = c_ref[...]
    ca = (c * _sigmoid(c)).astype(BF16)
    o_ref[0] = _dot(ca, w_ref[0].astype(BF16)) + b_ref[0]


def _mod_all(c_pad, w_mod, b_mod):
    depth, d, n = w_mod.shape
    tn = 1024
    rows = c_pad.shape[0]
    return pl.pallas_call(
        _mod_kernel,
        grid=(depth, n // tn),
        in_specs=[pl.BlockSpec((rows, d), lambda l, j: (0, 0)),
                  pl.BlockSpec((1, d, tn), lambda l, j: (l, 0, j)),
                  pl.BlockSpec((1, 1, tn), lambda l, j: (l, 0, j))],
        out_specs=pl.BlockSpec((1, rows, tn), lambda l, j: (l, 0, j)),
        out_shape=jax.ShapeDtypeStruct((depth, rows, n), F32),
        compiler_params=_cparams(("arbitrary", "arbitrary")),
        name="adaln_mod",
    )(c_pad, w_mod, b_mod.reshape(depth, 1, n))


def _modulate_kernel(x_ref, sc_ref, sh_ref, u_ref):
    u_ref[...] = (x_ref[...] * (1.0 + sc_ref[0]) + sh_ref[0]).astype(u_ref.dtype)


def _modulate(x2, sc, sh, seq):
    m, d = x2.shape
    tm = 1024
    per_b = seq // tm
    vec = pl.BlockSpec((1, 1, d), lambda i: (i // per_b, 0, 0))
    return pl.pallas_call(
        _modulate_kernel,
        grid=(m // tm,),
        in_specs=[pl.BlockSpec((tm, d), lambda i: (i, 0)), vec, vec],
        out_specs=pl.BlockSpec((tm, d), lambda i: (i, 0)),
        out_shape=jax.ShapeDtypeStruct((m, d), BF16),
        compiler_params=_cparams(("arbitrary",)),
        name="modulate0",
    )(x2, sc, sh)


def _mm_kernel(x_ref, w_ref, o_ref):
    o_ref[...] = _dot(x_ref[...], w_ref[...]).astype(o_ref.dtype)


def _matmul(x, w, out_dtype, tm, tn, name):
    m, k = x.shape
    n = w.shape[1]
    return pl.pallas_call(
        _mm_kernel,
        grid=(n // tn, m // tm),
        in_specs=[pl.BlockSpec((tm, k), lambda j, i: (i, 0)),
                  pl.BlockSpec((k, tn), lambda j, i: (0, j))],
        out_specs=pl.BlockSpec((tm, tn), lambda j, i: (i, j)),
        out_shape=jax.ShapeDtypeStruct((m, n), out_dtype),
        compiler_params=_cparams(("arbitrary", "arbitrary")),
        name=name,
    )(x, w)


def _mlstm_kernel(qk_ref, v_ref, og_ref, u_ref, wgc_ref, wgr_ref, cw_ref, cb_ref, bc_ref, br_ref, nw_ref,
                  tri_ref, trit_ref, y_ref, xp_ref, c_ref, n_ref, m_ref, *, L):
    dqk_all = M_HEADS * M_DQK

    @pl.when(pl.program_id(1) == 0)
    def _():
        xp_ref[0:CONV_HALO, :] = jnp.zeros((CONV_HALO, 2 * dqk_all), F32)
        c_ref[...] = jnp.zeros(c_ref.shape, F32)
        n_ref[...] = jnp.zeros(n_ref.shape, F32)
        m_ref[...] = jnp.zeros(m_ref.shape, F32)

    x = qk_ref[...]
    xp_ref[CONV_HALO:CONV_HALO + L, :] = x
    y = cb_ref[...]
    for j in range(CONV_W):
        off = CONV_HALO - (CONV_W - 1) + j
        y = y + cw_ref[j:j + 1, :] * xp_ref[off:off + L, :]
    xp_ref[0:CONV_HALO, :] = x[L - CONV_HALO:L, :]
    qk = y * _sigmoid(y)

    u = u_ref[...]
    gc = _dot(u, wgc_ref[...]) + bc_ref[...]
    gr = _dot_nt(wgr_ref[...], u)[0:2 * M_HEADS, :] + br_ref[...]
    bcum_c = _dot(tri_ref[...], jnp.concatenate(_split3(_log_sigmoid(gc) * LOG2E), axis=0))
    bcum_r = _dot(jnp.concatenate(_split3(_log_sigmoid(gr) * LOG2E), axis=1), trit_ref[...])
    gc = gc * LOG2E
    gr = gr * LOG2E

    ti = lax.broadcasted_iota(jnp.int32, (L, L), 0)
    si = lax.broadcasted_iota(jnp.int32, (L, L), 1)
    causal = si <= ti

    for h in range(M_HEADS):
        q = qk[:, h * M_DQK:(h + 1) * M_DQK] * (M_DQK ** -0.5)
        k = qk[:, dqk_all + h * M_DQK: dqk_all + (h + 1) * M_DQK]
        qb = q.astype(BF16)
        kb = k.astype(BF16)
        vb = v_ref[:, h * M_DV:(h + 1) * M_DV].astype(BF16)
        bc = bcum_c[:, M_HEADS + h:M_HEADS + h + 1]
        ic = gc[:, h:h + 1]
        br = bcum_r[M_HEADS + h:M_HEADS + h + 1, :]
        ir = gr[h:h + 1, :]
        c_prev = c_ref[h]
        n_prev = n_ref[h:h + 1, :]
        m_prev = m_ref[h:h + 1, 0:1]

        log_d = jnp.where(causal, bc - br + ir, -jnp.inf)
        m_inter = bc + m_prev
        m_t = jnp.maximum(m_inter, jnp.max(log_d, axis=1, keepdims=True))
        dw = jnp.exp2(log_d - m_t)
        w_inter = jnp.exp2(m_inter - m_t)
        s = _dot_nt(qb, kb) * dw
        num = _dot(s.astype(BF16), vb) + w_inter * _dot(qb, c_prev.astype(BF16))
        den = jnp.sum(s, axis=1, keepdims=True) + w_inter * jnp.sum(q * n_prev, axis=1, keepdims=True)
        hh = num * (1.0 / jnp.maximum(jnp.abs(den), jnp.exp2(-m_t)))

        b_last = bc[L - 1:L, :]
        lwe = b_last - bc + ic
        m_new = jnp.maximum(b_last + m_prev, jnp.max(lwe, axis=0, keepdims=True))
        w_end = jnp.exp2(lwe - m_new)
        decay = jnp.exp2(b_last + m_prev - m_new)
        kw = k * w_end
        c_ref[h] = decay * c_prev + _dot_tn(kw.astype(BF16), vb)
        n_ref[h:h + 1, :] = decay * n_prev + jnp.sum(kw, axis=0, keepdims=True)
        m_ref[h:h + 1, :] = jnp.broadcast_to(m_new, (1, m_ref.shape[1]))

        hc = hh - jnp.mean(hh, axis=1, keepdims=True)
        hn = hc * lax.rsqrt(jnp.mean(hc * hc, axis=1, keepdims=True) + EPS)
        sl = slice(h * M_DV, (h + 1) * M_DV)
        y_ref[:, sl] = (hn * nw_ref[:, sl] * _sigmoid(og_ref[:, sl])).astype(y_ref.dtype)


def _tri_consts(L):
    tri = np.tril(np.ones((L, L), np.float32))
    tri3 = np.concatenate([tri, tri, tri], axis=1)
    trit3 = np.concatenate([tri.T, tri.T, tri.T], axis=0)
    return jnp.asarray(tri3, BF16), jnp.asarray(trit3, BF16)


def _mlstm(proj, u, w_gate_cols, conv_w, conv_b, b_ig, b_fg, norm_w, batch, seq):
    L = MLSTM_CHUNK
    nc = seq // L
    wq = 2 * M_HEADS * M_DQK
    tri3, trit3 = _tri_consts(L)
    bias_c = jnp.zeros((1, N_GATE), F32).at[0, :M_HEADS].set(b_ig).at[0, M_HEADS:2 * M_HEADS].set(b_fg)
    bias_r = jnp.concatenate([b_ig, b_fg]).reshape(2 * M_HEADS, 1)
    d = u.shape[1]
    wgc = jnp.pad(w_gate_cols, ((0, 0), (0, N_GATE - 2 * M_HEADS))).astype(BF16)
    wgr = jnp.pad(w_gate_cols.T, ((0, BF16_ROWS - 2 * M_HEADS), (0, 0))).astype(BF16)
    row = lambda b, c: (b * nc + c, 0)
    const = lambda b, c: (0, 0)
    return pl.pallas_call(
        functools.partial(_mlstm_kernel, L=L),
        grid=(batch, nc),
        in_specs=[pl.BlockSpec((L, wq), lambda b, c: (b * nc + c, 0)),
                  pl.BlockSpec((L, D_MLSTM), lambda b, c: (b * nc + c, 1)),
                  pl.BlockSpec((L, D_MLSTM), lambda b, c: (b * nc + c, 2)),
                  pl.BlockSpec((L, d), row),
                  pl.BlockSpec((d, N_GATE), const),
                  pl.BlockSpec((BF16_ROWS, d), const),
                  pl.BlockSpec((CONV_W, wq), const),
                  pl.BlockSpec((1, wq), const),
                  pl.BlockSpec((1, N_GATE), const),
                  pl.BlockSpec((2 * M_HEADS, 1), const),
                  pl.BlockSpec((1, D_MLSTM), const),
                  pl.BlockSpec((L, 3 * L), const),
                  pl.BlockSpec((3 * L, L), const)],
        out_specs=pl.BlockSpec((L, D_MLSTM), row),
        out_shape=jax.ShapeDtypeStruct((batch * seq, D_MLSTM), BF16),
        scratch_shapes=[pltpu.VMEM((CONV_HALO + L, wq), F32),
                        pltpu.VMEM((M_HEADS, M_DQK, M_DV), F32),
                        pltpu.VMEM((8, M_DQK), F32),
                        pltpu.VMEM((8, 128), F32)],
        compiler_params=_cparams(("arbitrary", "arbitrary")),
        name="mlstm",
    )(proj, proj, proj, u, wgc, wgr, conv_w, conv_b.reshape(1, wq), bias_c, bias_r,
      norm_w.reshape(1, D_MLSTM), tri3, trit3)


def _hgrn_kernel(q_ref, f_ref, i_ref, g_ref, lbl_ref, nw_ref, tri_ref, y_ref, st_ref, b_ref, *, L, layer):
    dh_all = H_HEADS * H_DH
    zero = lambda n: jnp.zeros((n, dh_all), F32)

    @pl.when(pl.program_id(1) == 0)
    def _():
        st_ref[...] = jnp.zeros(st_ref.shape, F32)

    qf = q_ref[...]
    f = f_ref[...]
    qs = qf * _sigmoid(qf)
    e = jnp.exp(-jnp.abs(f))
    d = 1.0 + e
    log_f = jnp.minimum(f, 0.0) - jnp.log(d)
    kk = jnp.where(f >= 0.0, e, 1.0) / d
    if layer > 0:
        lg = lbl_ref[...]
        ex = jnp.exp(lg - jnp.max(lg, axis=0, keepdims=True))
        pr = ex / jnp.sum(ex, axis=0, keepdims=True)
        lb = pr[1:2, :]
        for j in range(2, layer + 1):
            lb = lb + pr[j:j + 1, :]
        c = jnp.log1p(-lb) + log_f
        a = jnp.log(lb)
        log_f = jnp.maximum(a, c) + jnp.log(1.0 + jnp.exp(-jnp.abs(a - c)))
        kk = (1.0 - lb) * kk

    lf2 = log_f * LOG2E
    b = _dot(tri_ref[...], jnp.concatenate(_split3(lf2), axis=0))
    b_ref[...] = b
    b_last = b[L - 1:L, :]
    qb = (qs * jnp.exp2(b)).astype(BF16)
    kd = (kk * jnp.exp2(b_last - b)).astype(BF16)
    dec = jnp.exp2(b_last)

    sub = lax.broadcasted_iota(jnp.int32, (SUBLANES, dh_all), 0)
    groups = range(L // SUBLANES)
    grp = lambda v, g: v[g * SUBLANES:(g + 1) * SUBLANES]
    odd = (sub & 1) != 0
    f2 = jnp.exp2(lf2)
    q_lv = [qs.astype(BF16),
            jnp.concatenate([jnp.where(odd, grp(qs, g) * grp(f2, g), 0.0) for g in groups], axis=0).astype(BF16)]
    k_lv = [kk.astype(BF16),
            jnp.concatenate([jnp.where(odd, 0.0, grp(kk, g)) for g in groups], axis=0).astype(BF16)]
    half = 2
    while half < L:
        qp, kp = [], []
        if half < SUBLANES:
            upper = (sub & half) != 0
            for g in groups:
                r = None
                for mblk in range(SUBLANES // (2 * half)):
                    cand = jnp.broadcast_to(b_ref[pl.ds(g * SUBLANES + mblk * 2 * half + half - 1, 1), :],
                                            (SUBLANES, dh_all))
                    r = cand if r is None else jnp.where(sub >= mblk * 2 * half, cand, r)
                fac = jnp.exp2(-jnp.abs(grp(b, g) - r))
                qp.append(jnp.where(upper, grp(qs, g) * fac, 0.0))
                kp.append(jnp.where(upper, 0.0, grp(kk, g) * fac))
        else:
            for j in range(L // (2 * half)):
                lo = slice(j * 2 * half, j * 2 * half + half)
                up = slice(j * 2 * half + half, (j + 1) * 2 * half)
                r = b_ref[pl.ds(j * 2 * half + half - 1, 1), :]
                kp += [kk[lo] * jnp.exp2(r - b[lo]), zero(half)]
                qp += [zero(half), qs[up] * jnp.exp2(b[up] - r)]
        q_lv.append(jnp.concatenate(qp, axis=0).astype(BF16))
        k_lv.append(jnp.concatenate(kp, axis=0).astype(BF16))
        half *= 2

    ti = lax.broadcasted_iota(jnp.int32, (L, L), 0)
    si = lax.broadcasted_iota(jnp.int32, (L, L), 1)
    tx = ti ^ si

    atts = []
    for h in range(H_HEADS):
        sl = slice(h * H_DH, (h + 1) * H_DH)
        att = _dot_nt(q_lv[-1][:, sl], k_lv[-1][:, sl])
        for lev in range(len(q_lv) - 2, -1, -1):
            att = jnp.where(tx < (1 << lev), _dot_nt(q_lv[lev][:, sl], k_lv[lev][:, sl]), att)
        atts.append(att.astype(BF16))

    for h in range(H_HEADS):
        sl = slice(h * H_DH, (h + 1) * H_DH)
        vb = i_ref[:, sl].astype(BF16)
        st = st_ref[h]
        o = _dot(atts[h], vb) + _dot_nt(qb[:, sl], st.astype(BF16))
        st_ref[h] = st * dec[:, sl] + _dot_tn(vb, kd[:, sl])
        on = o * lax.rsqrt(jnp.mean(o * o, axis=1, keepdims=True) + EPS)
        g = g_ref[:, sl]
        y_ref[:, sl] = (on * nw_ref[:, sl] * (g * _sigmoid(g))).astype(y_ref.dtype)


def _hgrn(proj, lb_logits, norm_w, layer, batch, seq):
    L = HGRN_CHUNK
    nc = seq // L
    tri3, _ = _tri_consts(L)
    const = lambda b, c: (0, 0)
    col = lambda j: pl.BlockSpec((L, D_HGRN), lambda b, c: (b * nc + c, j))
    return pl.pallas_call(
        functools.partial(_hgrn_kernel, L=L, layer=layer),
        grid=(batch, nc),
        in_specs=[col(3), col(4), col(5), col(6),
                  pl.BlockSpec(lb_logits.shape, const),
                  pl.BlockSpec((1, D_HGRN), const),
                  pl.BlockSpec(tri3.shape, const)],
        out_specs=pl.BlockSpec((L, D_HGRN), lambda b, c: (b * nc + c, 0)),
        out_shape=jax.ShapeDtypeStruct((batch * seq, D_HGRN), BF16),
        scratch_shapes=[pltpu.VMEM((H_HEADS, H_DH, H_DH), F32), pltpu.VMEM((L, D_HGRN), F32)],
        compiler_params=_cparams(("arbitrary", "arbitrary")),
        name="hgrn2",
    )(proj, proj, proj, proj, lb_logits, norm_w.reshape(1, D_HGRN), tri3)


def _matmul_residual_ln(ychunk, x_ref, gate_ref, lng_ref, lnb_ref, sc_ref, sh_ref, xo_ref, uo_ref, z_ref,
                        *, sub, cw):
    tm, d = z_ref.shape
    nsub, nchunk = tm // sub, d // cw
    gp = (1.0 + gate_ref[0]) * (1.0 / ALPHA)
    lng, lnb = lng_ref[...], lnb_ref[...]
    if uo_ref is not None:
        ug = lng * (1.0 + sc_ref[0])
        ub = lnb * (1.0 + sc_ref[0]) + sh_ref[0]
    mu = None
    for s in range(nsub + 1):
        rows = pl.ds(s * sub, sub)
        prev = pl.ds((s - 1) * sub, sub)
        rsum = None
        for j in range(nchunk):
            cols = slice(j * cw, (j + 1) * cw)
            if s < nsub:
                z = x_ref[rows, cols] + gp[:, cols] * ychunk(rows, cols)
                z_ref[rows, cols] = z
                part = jnp.sum(z, axis=1, keepdims=True)
                rsum = part if rsum is None else rsum + part
            if s > 0:
                if j == 0:
                    zc = z_ref[prev, :] - mu
                    rstd = lax.rsqrt(jnp.mean(zc * zc, axis=1, keepdims=True) + EPS / (ALPHA * ALPHA))
                t = (z_ref[prev, cols] - mu) * rstd
                xo_ref[prev, cols] = t * lng[:, cols] + lnb[:, cols]
                if uo_ref is not None:
                    uo_ref[prev, cols] = (t * ug[:, cols] + ub[:, cols]).astype(uo_ref.dtype)
        if s < nsub:
            mu = rsum * (1.0 / d)


def _outproj_kernel(ym_ref, yh_ref, w_ref, x_ref, gate_ref, lng_ref, lnb_ref, sc_ref, sh_ref, xo_ref, uo_ref,
                    wb_ref, z_ref, *, sub, cw):
    @pl.when(pl.program_id(0) == 0)
    def _():
        wb_ref[...] = w_ref[...].astype(BF16)

    def ychunk(rows, cols):
        return (_dot(ym_ref[rows, :], wb_ref[0:D_MLSTM, cols]) +
                _dot(yh_ref[rows, :], wb_ref[D_MLSTM:D_MLSTM + D_HGRN, cols]))

    _matmul_residual_ln(ychunk, x_ref, gate_ref, lng_ref, lnb_ref, sc_ref, sh_ref, xo_ref, uo_ref, z_ref,
                        sub=sub, cw=cw)


def _outproj_ln(ym, yh, w_all, layer, x2, gate, ln_g, ln_b, sc, sh, seq):
    m, d = x2.shape
    wshape = w_all.shape[1:]
    tm, sub, cw = 512, 128, 512
    per_b = seq // tm
    vec = pl.BlockSpec((1, 1, d), lambda i: (i // per_b, 0, 0))
    par = pl.BlockSpec((1, d), lambda i: (0, 0))
    rowblk = lambda width: pl.BlockSpec((tm, width), lambda i: (i, 0))
    return pl.pallas_call(
        functools.partial(_outproj_kernel, sub=sub, cw=cw),
        grid=(m // tm,),
        in_specs=[rowblk(D_MLSTM), rowblk(D_HGRN),
                  pl.BlockSpec((None,) + wshape, lambda i: (layer, 0, 0), pipeline_mode=pl.Buffered(1)),
                  rowblk(d), vec, par, par, vec, vec],
        out_specs=[rowblk(d), rowblk(d)],
        out_shape=[jax.ShapeDtypeStruct((m, d), F32), jax.ShapeDtypeStruct((m, d), BF16)],
        scratch_shapes=[pltpu.VMEM(wshape, BF16), pltpu.VMEM((tm, d), F32)],
        compiler_params=_cparams(("arbitrary",)),
        name="outproj_ln",
    )(ym, yh, w_all, x2, gate, ln_g.reshape(1, d), ln_b.reshape(1, d), sc, sh)


def _ffn_up_kernel(u_ref, wg_ref, wu_ref, h_ref, wgb_ref, wub_ref, *, cw):
    @pl.when(pl.program_id(1) == 0)
    def _():
        wgb_ref[...] = wg_ref[...].astype(BF16)
        wub_ref[...] = wu_ref[...].astype(BF16)

    u = u_ref[...]
    for j in range(h_ref.shape[1] // cw):
        cols = slice(j * cw, (j + 1) * cw)
        g = _dot(u, wgb_ref[:, cols])
        h_ref[:, cols] = (g * _sigmoid(g) * _dot(u, wub_ref[:, cols])).astype(h_ref.dtype)


def _ffn_up(u, wg_all, wu_all, layer):
    m, k = u.shape
    n = wg_all.shape[2]
    tm, tn, cw = 2048, 512, 256
    wspec = pl.BlockSpec((None, k, tn), lambda j, i: (layer, 0, j))
    return pl.pallas_call(
        functools.partial(_ffn_up_kernel, cw=cw),
        grid=(n // tn, m // tm),
        in_specs=[pl.BlockSpec((tm, k), lambda j, i: (i, 0)), wspec, wspec],
        out_specs=pl.BlockSpec((tm, tn), lambda j, i: (i, j)),
        out_shape=jax.ShapeDtypeStruct((m, n), BF16),
        scratch_shapes=[pltpu.VMEM((k, tn), BF16), pltpu.VMEM((k, tn), BF16)],
        compiler_params=_cparams(("arbitrary", "arbitrary")),
        name="ffn_up",
    )(u, wg_all, wu_all)


def _down_kernel(h_ref, w_ref, x_ref, gate_ref, lng_ref, lnb_ref, sc_ref, sh_ref, *refs, sub, cw, emit_u):
    xo_ref = refs[0]
    uo_ref = refs[1] if emit_u else None
    z_ref = refs[-1]
    ychunk = lambda rows, cols: _dot(h_ref[rows, :], w_ref[:, cols])
    _matmul_residual_ln(ychunk, x_ref, gate_ref, lng_ref, lnb_ref, sc_ref, sh_ref, xo_ref, uo_ref, z_ref,
                        sub=sub, cw=cw)


def _down_ln(hid, w, x2, gate, ln_g, ln_b, sc, sh, seq, emit_u):
    m, d = x2.shape
    kdim = hid.shape[1]
    tm, sub, cw = 256, 128, 512
    per_b = seq // tm
    vec = pl.BlockSpec((1, 1, d), lambda i: (i // per_b, 0, 0))
    par = pl.BlockSpec((1, d), lambda i: (0, 0))
    rowblk = pl.BlockSpec((tm, d), lambda i: (i, 0))
    n_out = 2 if emit_u else 1
    return pl.pallas_call(
        functools.partial(_down_kernel, sub=sub, cw=cw, emit_u=emit_u),
        grid=(m // tm,),
        in_specs=[pl.BlockSpec((tm, kdim), lambda i: (i, 0)),
                  pl.BlockSpec(w.shape, lambda i: (0, 0), pipeline_mode=pl.Buffered(1)),
                  rowblk, vec, par, par, vec, vec],
        out_specs=[rowblk, rowblk][:n_out],
        out_shape=[jax.ShapeDtypeStruct((m, d), F32), jax.ShapeDtypeStruct((m, d), BF16)][:n_out],
        scratch_shapes=[pltpu.VMEM((tm, d), F32)],
        compiler_params=_cparams(("arbitrary",)),
        name="ffn_down_ln",
    )(hid, w, x2, gate, ln_g.reshape(1, d), ln_b.reshape(1, d), sc, sh)


def _split_w_in(w_in_l):
    g0 = 3 * 1024
    g1 = g0 + 2 * M_HEADS
    big = jnp.concatenate([w_in_l[:, :g0], w_in_l[:, g1:]], axis=1).astype(BF16)
    return big, w_in_l[:, g0:g1]


def kernel(x, c, w_mod, b_mod, w_in, conv_w, conv_b, b_igate, b_fgate, mlstm_norm_w, hgrn_norm_w, lb_logits, w_out, ln1_g, ln1_b, w_gate, w_up, w_down, ln2_g, ln2_b):
    batch, seq, d = x.shape
    depth = w_mod.shape[0]
    m = batch * seq
    x2 = x.reshape(m, d)

    c_pad = jnp.pad(c, ((0, 8 - batch), (0, 0)))
    mod = _mod_all(c_pad, w_mod, b_mod)[:, :batch]
    mod = mod.reshape(depth, batch, N_MOD, 1, d)
    mvec = lambda l, k: mod[l, :, k]

    u = _modulate(x2, mvec(0, 1), mvec(0, 0), seq)
    for l in range(depth):
        w_big, w_g = _split_w_in(w_in[l])
        proj = _matmul(u, w_big, F32, 2048, 1024, "in_proj")
        ym = _mlstm(proj, u, w_g, conv_w[l], conv_b[l], b_igate[l], b_fgate[l], mlstm_norm_w[l],
                    batch, seq)
        yh = _hgrn(proj, lb_logits, hgrn_norm_w[l], l, batch, seq)
        x2, u = _outproj_ln(ym, yh, w_out, l, x2, mvec(l, 2), ln1_g[l], ln1_b[l], mvec(l, 4), mvec(l, 3), seq)
        hid = _ffn_up(u, w_gate, w_up, l)
        last = l == depth - 1
        nl = l if last else l + 1
        outs = _down_ln(hid, w_down[l].astype(BF16), x2, mvec(l, 5), ln2_g[l], ln2_b[l],
                        mvec(nl, 1), mvec(nl, 0), seq, not last)
        x2 = outs[0]
        u = None if last else outs[1]
    return x2.reshape(batch, seq, d)
```

```python
import functools
import math

import numpy as np
import jax
import jax.numpy as jnp
from jax import lax
from jax.experimental import pallas as pl
from jax.experimental.pallas import tpu as pltpu

F32 = jnp.float32
BF16 = jnp.bfloat16

D_MODEL = 2048
DEPTH = 4
D_MLSTM = 1024
D_HGRN = 1024
M_HEADS = 4
M_DV = 256
M_DQK = 128
H_HEADS = 8
H_DH = 128
CONV_W = 4
D_FF = 5632
N_MOD = 6
EPS = 1e-5
ALPHA = (2 * DEPTH) ** 0.25
N_GATE = 128
SUBLANES = 8
BF16_ROWS = 16
CONV_HALO = SUBLANES
LOG2E = 1.4426950408889634

VMEM_LIMIT = 56 * 1024 * 1024

MLSTM_CHUNK = 256
HGRN_CHUNK = 128
HGRN_BLOCK = 256


def _cparams(sem):
    return pltpu.CompilerParams(dimension_semantics=sem, vmem_limit_bytes=VMEM_LIMIT)


def _sigmoid(x):
    return 1.0 / (1.0 + jnp.exp(-x))


def _log_sigmoid(x):
    return jnp.minimum(x, 0.0) - jnp.log(1.0 + jnp.exp(-jnp.abs(x)))


def _split3(x):
    hi = x.astype(BF16)
    r1 = x - hi.astype(F32)
    mid = r1.astype(BF16)
    lo = (r1 - mid.astype(F32)).astype(BF16)
    return hi, mid, lo


def _dot(a, b):
    return jnp.dot(a, b, preferred_element_type=F32)


def _dot_nt(a, b):
    return lax.dot_general(a, b, (((1,), (1,)), ((), ())), preferred_element_type=F32)


def _dot_tn(a, b):
    return lax.dot_general(a, b, (((0,), (0,)), ((), ())), preferred_element_type=F32)


def _mod_kernel(c_ref, w_ref, b_ref, o_ref):
    c = c_ref[...]
    ca = (c * _sigmoid(c)).astype(BF16)
    o_ref[0] = _dot(ca, w_ref[0].astype(BF16)) + b_ref[0]


def _mod_all(c_pad, w_mod, b_mod):
    depth, d, n = w_mod.shape
    tn = 1024
    rows = c_pad.shape[0]
    return pl.pallas_call(
        _mod_kernel,
        grid=(depth, n // tn),
        in_specs=[pl.BlockSpec((rows, d), lambda l, j: (0, 0)),
                  pl.BlockSpec((1, d, tn), lambda l, j: (l, 0, j)),
                  pl.BlockSpec((1, 1, tn), lambda l, j: (l, 0, j))],
        out_specs=pl.BlockSpec((1, rows, tn), lambda l, j: (l, 0, j)),
        out_shape=jax.ShapeDtypeStruct((depth, rows, n), F32),
        compiler_params=_cparams(("arbitrary", "arbitrary")),
        name="adaln_mod",
    )(c_pad, w_mod, b_mod.reshape(depth, 1, n))


def _modulate_kernel(x_ref, sc_ref, sh_ref, u_ref):
    u_ref[...] = (x_ref[...] * (1.0 + sc_ref[0]) + sh_ref[0]).astype(u_ref.dtype)


def _modulate(x2, sc, sh, seq):
    m, d = x2.shape
    tm = 1024
    per_b = seq // tm
    vec = pl.BlockSpec((1, 1, d), lambda i: (i // per_b, 0, 0))
    return pl.pallas_call(
        _modulate_kernel,
        grid=(m // tm,),
        in_specs=[pl.BlockSpec((tm, d), lambda i: (i, 0)), vec, vec],
        out_specs=pl.BlockSpec((tm, d), lambda i: (i, 0)),
        out_shape=jax.ShapeDtypeStruct((m, d), BF16),
        compiler_params=_cparams(("arbitrary",)),
        name="modulate0",
    )(x2, sc, sh)


def _mm_kernel(x_ref, w_ref, o_ref):
    o_ref[...] = _dot(x_ref[...], w_ref[...]).astype(o_ref.dtype)


def _matmul(x, w_all, layer, out_dtype, tm, tn, name):
    m, k = x.shape
    n = w_all.shape[2]
    return pl.pallas_call(
        _mm_kernel,
        grid=(n // tn, m // tm),
        in_specs=[pl.BlockSpec((tm, k), lambda j, i: (i, 0)),
                  pl.BlockSpec((None, k, tn), lambda j, i: (layer, 0, j))],
        out_specs=pl.BlockSpec((tm, tn), lambda j, i: (i, j)),
        out_shape=jax.ShapeDtypeStruct((m, n), out_dtype),
        compiler_params=_cparams(("arbitrary", "arbitrary")),
        name=name,
    )(x, w_all)


def _mlstm_kernel(qk_ref, v_ref, og_ref, u_ref, wgc_ref, wgr_ref, cw_ref, cb_ref, bc_ref, br_ref, nw_ref,
                  tri_ref, trit_ref, y_ref, xp_ref, c_ref, n_ref, m_ref, *, L):
    dqk_all = M_HEADS * M_DQK

    @pl.when(pl.program_id(1) == 0)
    def _():
        xp_ref[0:CONV_HALO, :] = jnp.zeros((CONV_HALO, 2 * dqk_all), F32)
        c_ref[...] = jnp.zeros(c_ref.shape, F32)
        n_ref[...] = jnp.zeros(n_ref.shape, F32)
        m_ref[...] = jnp.zeros(m_ref.shape, F32)

    x = qk_ref[...]
    xp_ref[CONV_HALO:CONV_HALO + L, :] = x
    y = cb_ref[...]
    for j in range(CONV_W):
        off = CONV_HALO - (CONV_W - 1) + j
        y = y + cw_ref[j:j + 1, :] * xp_ref[off:off + L, :]
    xp_ref[0:CONV_HALO, :] = x[L - CONV_HALO:L, :]
    qk = y * _sigmoid(y)

    u = u_ref[...]
    gc = _dot(u, wgc_ref[...]) + bc_ref[...]
    gr = _dot_nt(wgr_ref[...], u)[0:2 * M_HEADS, :] + br_ref[...]
    bcum_c = _dot(tri_ref[...], jnp.concatenate(_split3(_log_sigmoid(gc) * LOG2E), axis=0))
    bcum_r = _dot(jnp.concatenate(_split3(_log_sigmoid(gr) * LOG2E), axis=1), trit_ref[...])
    gc = gc * LOG2E
    gr = gr * LOG2E

    ti = lax.broadcasted_iota(jnp.int32, (L, L), 0)
    si = lax.broadcasted_iota(jnp.int32, (L, L), 1)
    causal = si <= ti

    for h in range(M_HEADS):
        q = qk[:, h * M_DQK:(h + 1) * M_DQK] * (M_DQK ** -0.5)
        k = qk[:, dqk_all + h * M_DQK: dqk_all + (h + 1) * M_DQK]
        qb = q.astype(BF16)
        kb = k.astype(BF16)
        vb = v_ref[:, h * M_DV:(h + 1) * M_DV].astype(BF16)
        bc = bcum_c[:, M_HEADS + h:M_HEADS + h + 1]
        ic = gc[:, h:h + 1]
        br = bcum_r[M_HEADS + h:M_HEADS + h + 1, :]
        ir = gr[h:h + 1, :]
        c_prev = c_ref[h]
        n_prev = n_ref[h:h + 1, :]
        m_prev = m_ref[h:h + 1, 0:1]

        log_d = jnp.where(causal, bc - br + ir, -jnp.inf)
        m_inter = bc + m_prev
        m_t = jnp.maximum(m_inter, jnp.max(log_d, axis=1, keepdims=True))
        dw = jnp.exp2(log_d - m_t)
        w_inter = jnp.exp2(m_inter - m_t)
        s = _dot_nt(qb, kb) * dw
        num = _dot(s.astype(BF16), vb) + w_inter * _dot(qb, c_prev.astype(BF16))
        den = jnp.sum(s, axis=1, keepdims=True) + w_inter * jnp.sum(q * n_prev, axis=1, keepdims=True)
        hh = num * (1.0 / jnp.maximum(jnp.abs(den), jnp.exp2(-m_t)))

        b_last = bc[L - 1:L, :]
        lwe = b_last - bc + ic
        m_new = jnp.maximum(b_last + m_prev, jnp.max(lwe, axis=0, keepdims=True))
        w_end = jnp.exp2(lwe - m_new)
        decay = jnp.exp2(b_last + m_prev - m_new)
        kw = k * w_end
        c_ref[h] = decay * c_prev + _dot_tn(kw.astype(BF16), vb)
        n_ref[h:h + 1, :] = decay * n_prev + jnp.sum(kw, axis=0, keepdims=True)
        m_ref[h:h + 1, :] = jnp.broadcast_to(m_new, (1, m_ref.shape[1]))

        hc = hh - jnp.mean(hh, axis=1, keepdims=True)
        hn = hc * lax.rsqrt(jnp.mean(hc * hc, axis=1, keepdims=True) + EPS)
        sl = slice(h * M_DV, (h + 1) * M_DV)
        y_ref[:, sl] = (hn * nw_ref[:, sl] * _sigmoid(og_ref[:, sl])).astype(y_ref.dtype)


def _tri_consts(L):
    tri = np.tril(np.ones((L, L), np.float32))
    tri3 = np.concatenate([tri, tri, tri], axis=1)
    trit3 = np.concatenate([tri.T, tri.T, tri.T], axis=0)
    return jnp.asarray(tri3, BF16), jnp.asarray(trit3, BF16)


def _mlstm(proj, u, wgc_all, wgr_all, layer, conv_w, conv_b, b_ig, b_fg, norm_w, batch, seq):
    L = MLSTM_CHUNK
    nc = seq // L
    wq = 2 * M_HEADS * M_DQK
    tri3, trit3 = _tri_consts(L)
    bias_c = jnp.zeros((1, N_GATE), F32).at[0, :M_HEADS].set(b_ig).at[0, M_HEADS:2 * M_HEADS].set(b_fg)
    bias_r = jnp.concatenate([b_ig, b_fg]).reshape(2 * M_HEADS, 1)
    d = u.shape[1]
    row = lambda b, c: (b * nc + c, 0)
    const = lambda b, c: (0, 0)
    return pl.pallas_call(
        functools.partial(_mlstm_kernel, L=L),
        grid=(batch, nc),
        in_specs=[pl.BlockSpec((L, wq), lambda b, c: (b * nc + c, 0)),
                  pl.BlockSpec((L, D_MLSTM), lambda b, c: (b * nc + c, 1)),
                  pl.BlockSpec((L, D_MLSTM), lambda b, c: (b * nc + c, 2)),
                  pl.BlockSpec((L, d), row),
                  pl.BlockSpec((None, d, N_GATE), lambda b, c: (layer, 0, 0)),
                  pl.BlockSpec((None, BF16_ROWS, d), lambda b, c: (layer, 0, 0)),
                  pl.BlockSpec((CONV_W, wq), const),
                  pl.BlockSpec((1, wq), const),
                  pl.BlockSpec((1, N_GATE), const),
                  pl.BlockSpec((2 * M_HEADS, 1), const),
                  pl.BlockSpec((1, D_MLSTM), const),
                  pl.BlockSpec((L, 3 * L), const),
                  pl.BlockSpec((3 * L, L), const)],
        out_specs=pl.BlockSpec((L, D_MLSTM), row),
        out_shape=jax.ShapeDtypeStruct((batch * seq, D_MLSTM), BF16),
        scratch_shapes=[pltpu.VMEM((CONV_HALO + L, wq), F32),
                        pltpu.VMEM((M_HEADS, M_DQK, M_DV), F32),
                        pltpu.VMEM((8, M_DQK), F32),
                        pltpu.VMEM((8, 128), F32)],
        compiler_params=_cparams(("arbitrary", "arbitrary")),
        name="mlstm",
    )(proj, proj, proj, u, wgc_all, wgr_all, conv_w, conv_b.reshape(1, wq), bias_c, bias_r,
      norm_w.reshape(1, D_MLSTM), tri3, trit3)


def _hgrn_chunk(q_ref, f_ref, i_ref, g_ref, lbl_ref, nw_ref, tri_ref, y_ref, st_ref, b_ref, *, L, layer):
    dh_all = H_HEADS * H_DH
    zero = lambda n: jnp.zeros((n, dh_all), F32)

    qf = q_ref[...]
    f = f_ref[...]
    qs = qf * _sigmoid(qf)
    e = jnp.exp(-jnp.abs(f))
    d = 1.0 + e
    log_f = jnp.minimum(f, 0.0) - jnp.log(d)
    kk = jnp.where(f >= 0.0, e, 1.0) / d
    if layer > 0:
        lg = lbl_ref[...]
        ex = jnp.exp(lg - jnp.max(lg, axis=0, keepdims=True))
        pr = ex / jnp.sum(ex, axis=0, keepdims=True)
        lb = pr[1:2, :]
        for j in range(2, layer + 1):
            lb = lb + pr[j:j + 1, :]
        c = jnp.log1p(-lb) + log_f
        a = jnp.log(lb)
        log_f = jnp.maximum(a, c) + jnp.log(1.0 + jnp.exp(-jnp.abs(a - c)))
        kk = (1.0 - lb) * kk

    lf2 = log_f * LOG2E
    b = _dot(tri_ref[...], jnp.concatenate(_split3(lf2), axis=0))
    b_ref[...] = b
    b_last = b[L - 1:L, :]
    qb = (qs * jnp.exp2(b)).astype(BF16)
    kd = (kk * jnp.exp2(b_last - b)).astype(BF16)
    dec = jnp.exp2(b_last)

    sub = lax.broadcasted_iota(jnp.int32, (SUBLANES, dh_all), 0)
    groups = range(L // SUBLANES)
    grp = lambda v, g: v[g * SUBLANES:(g + 1) * SUBLANES]
    odd = (sub & 1) != 0
    f2 = jnp.exp2(lf2)
    q_lv = [qs.astype(BF16),
            jnp.concatenate([jnp.where(odd, grp(qs, g) * grp(f2, g), 0.0) for g in groups], axis=0).astype(BF16)]
    k_lv = [kk.astype(BF16),
            jnp.concatenate([jnp.where(odd, 0.0, grp(kk, g)) for g in groups], axis=0).astype(BF16)]
    half = 2
    while half < L:
        qp, kp = [], []
        if half < SUBLANES:
            upper = (sub & half) != 0
            for g in groups:
                r = None
                for mblk in range(SUBLANES // (2 * half)):
                    cand = jnp.broadcast_to(b_ref[pl.ds(g * SUBLANES + mblk * 2 * half + half - 1, 1), :],
                                            (SUBLANES, dh_all))
                    r = cand if r is None else jnp.where(sub >= mblk * 2 * half, cand, r)
                fac = jnp.exp2(-jnp.abs(grp(b, g) - r))
                qp.append(jnp.where(upper, grp(qs, g) * fac, 0.0))
                kp.append(jnp.where(upper, 0.0, grp(kk, g) * fac))
        else:
            for j in range(L // (2 * half)):
                lo = slice(j * 2 * half, j * 2 * half + half)
                up = slice(j * 2 * half + half, (j + 1) * 2 * half)
                r = b_ref[pl.ds(j * 2 * half + half - 1, 1), :]
                kp += [kk[lo] * jnp.exp2(r - b[lo]), zero(half)]
                qp += [zero(half), qs[up] * jnp.exp2(b[up] - r)]
        q_lv.append(jnp.concatenate(qp, axis=0).astype(BF16))
        k_lv.append(jnp.concatenate(kp, axis=0).astype(BF16))
        half *= 2

    ti = lax.broadcasted_iota(jnp.int32, (L, L), 0)
    si = lax.broadcasted_iota(jnp.int32, (L, L), 1)
    tx = ti ^ si

    atts = []
    for h in range(H_HEADS):
        sl = slice(h * H_DH, (h + 1) * H_DH)
        att = _dot_nt(q_lv[-1][:, sl], k_lv[-1][:, sl])
        for lev in range(len(q_lv) - 2, -1, -1):
            att = jnp.where(tx < (1 << lev), _dot_nt(q_lv[lev][:, sl], k_lv[lev][:, sl]), att)
        atts.append(att.astype(BF16))

    for h in range(H_HEADS):
        sl = slice(h * H_DH, (h + 1) * H_DH)
        vb = i_ref[:, sl].astype(BF16)
        st = st_ref[h]
        o = _dot(atts[h], vb) + _dot_nt(qb[:, sl], st.astype(BF16))
        st_ref[h] = st * dec[:, sl] + _dot_tn(vb, kd[:, sl])
        on = o * lax.rsqrt(jnp.mean(o * o, axis=1, keepdims=True) + EPS)
        g = g_ref[:, sl]
        y_ref[:, sl] = (on * nw_ref[:, sl] * (g * _sigmoid(g))).astype(y_ref.dtype)


def _hgrn_kernel(q_ref, f_ref, i_ref, g_ref, lbl_ref, nw_ref, tri_ref, y_ref, st_ref, b_ref, *, L, layer):
    @pl.when(pl.program_id(1) == 0)
    def _():
        st_ref[...] = jnp.zeros(st_ref.shape, F32)

    for ck in range(q_ref.shape[0] // L):
        rows = pl.ds(ck * L, L)
        _hgrn_chunk(q_ref.at[rows], f_ref.at[rows], i_ref.at[rows], g_ref.at[rows], lbl_ref, nw_ref, tri_ref,
                    y_ref.at[rows], st_ref, b_ref.at[rows], L=L, layer=layer)


def _hgrn(proj, lb_logits, norm_w, layer, batch, seq):
    L = HGRN_CHUNK
    rows = HGRN_BLOCK
    nc = seq // rows
    tri3, _ = _tri_consts(L)
    const = lambda b, c: (0, 0)
    col = lambda j: pl.BlockSpec((rows, D_HGRN), lambda b, c: (b * nc + c, j))
    return pl.pallas_call(
        functools.partial(_hgrn_kernel, L=L, layer=layer),
        grid=(batch, nc),
        in_specs=[col(3), col(4), col(5), col(6),
                  pl.BlockSpec(lb_logits.shape, const),
                  pl.BlockSpec((1, D_HGRN), const),
                  pl.BlockSpec(tri3.shape, const)],
        out_specs=pl.BlockSpec((rows, D_HGRN), lambda b, c: (b * nc + c, 0)),
        out_shape=jax.ShapeDtypeStruct((batch * seq, D_HGRN), BF16),
        scratch_shapes=[pltpu.VMEM((H_HEADS, H_DH, H_DH), F32), pltpu.VMEM((rows, D_HGRN), F32)],
        compiler_params=_cparams(("arbitrary", "arbitrary")),
        name="hgrn2",
    )(proj, proj, proj, proj, lb_logits, norm_w.reshape(1, D_HGRN), tri3)


def _matmul_residual_ln(ychunk, x_ref, gate_ref, lng_ref, lnb_ref, sc_ref, sh_ref, xo_ref, uo_ref, z_ref,
                        *, sub, cw):
    tm, d = z_ref.shape
    nsub, nchunk = tm // sub, d // cw
    gp = (1.0 + gate_ref[0]) * (1.0 / ALPHA)
    lng, lnb = lng_ref[...], lnb_ref[...]
    if uo_ref is not None:
        ug = lng * (1.0 + sc_ref[0])
        ub = lnb * (1.0 + sc_ref[0]) + sh_ref[0]
    mu = None
    for s in range(nsub + 1):
        rows = pl.ds(s * sub, sub)
        prev = pl.ds((s - 1) * sub, sub)
        rsum = None
        for j in range(nchunk):
            cols = slice(j * cw, (j + 1) * cw)
            if s < nsub:
                z = x_ref[rows, cols] + gp[:, cols] * ychunk(rows, cols)
                z_ref[rows, cols] = z
                part = jnp.sum(z, axis=1, keepdims=True)
                rsum = part if rsum is None else rsum + part
            if s > 0:
                if j == 0:
                    zc = z_ref[prev, :] - mu
                    rstd = lax.rsqrt(jnp.mean(zc * zc, axis=1, keepdims=True) + EPS / (ALPHA * ALPHA))
                t = (z_ref[prev, cols] - mu) * rstd
                xo_ref[prev, cols] = t * lng[:, cols] + lnb[:, cols]
                if uo_ref is not None:
                    uo_ref[prev, cols] = (t * ug[:, cols] + ub[:, cols]).astype(uo_ref.dtype)
        if s < nsub:
            mu = rsum * (1.0 / d)


def _outproj_kernel(ym_ref, yh_ref, w_ref, x_ref, gate_ref, lng_ref, lnb_ref, sc_ref, sh_ref, xo_ref, uo_ref,
                    wb_ref, z_ref, *, sub, cw):
    @pl.when(pl.program_id(0) == 0)
    def _():
        wb_ref[...] = w_ref[...].astype(BF16)

    def ychunk(rows, cols):
        return (_dot(ym_ref[rows, :], wb_ref[0:D_MLSTM, cols]) +
                _dot(yh_ref[rows, :], wb_ref[D_MLSTM:D_MLSTM + D_HGRN, cols]))

    _matmul_residual_ln(ychunk, x_ref, gate_ref, lng_ref, lnb_ref, sc_ref, sh_ref, xo_ref, uo_ref, z_ref,
                        sub=sub, cw=cw)


def _outproj_ln(ym, yh, w_all, layer, x2, gate, ln_g, ln_b, sc, sh, seq):
    m, d = x2.shape
    wshape = w_all.shape[1:]
    tm, sub, cw = 512, 128, 512
    per_b = seq // tm
    vec = pl.BlockSpec((1, 1, d), lambda i: (i // per_b, 0, 0))
    par = pl.BlockSpec((1, d), lambda i: (0, 0))
    rowblk = lambda width: pl.BlockSpec((tm, width), lambda i: (i, 0))
    return pl.pallas_call(
        functools.partial(_outproj_kernel, sub=sub, cw=cw),
        grid=(m // tm,),
        in_specs=[rowblk(D_MLSTM), rowblk(D_HGRN),
                  pl.BlockSpec((None,) + wshape, lambda i: (layer, 0, 0), pipeline_mode=pl.Buffered(1)),
                  rowblk(d), vec, par, par, vec, vec],
        out_specs=[rowblk(d), rowblk(d)],
        out_shape=[jax.ShapeDtypeStruct((m, d), F32), jax.ShapeDtypeStruct((m, d), BF16)],
        scratch_shapes=[pltpu.VMEM(wshape, BF16), pltpu.VMEM((tm, d), F32)],
        compiler_params=_cparams(("arbitrary",)),
        name="outproj_ln",
    )(ym, yh, w_all, x2, gate, ln_g.reshape(1, d), ln_b.reshape(1, d), sc, sh)


def _ffn_up_kernel(u_ref, wg_ref, wu_ref, h_ref, wgb_ref, wub_ref, *, cw):
    @pl.when(pl.program_id(1) == 0)
    def _():
        wgb_ref[...] = wg_ref[...].astype(BF16)
        wub_ref[...] = wu_ref[...].astype(BF16)

    u = u_ref[...]
    for j in range(h_ref.shape[1] // cw):
        cols = slice(j * cw, (j + 1) * cw)
        g = _dot(u, wgb_ref[:, cols])
        h_ref[:, cols] = (g * _sigmoid(g) * _dot(u, wub_ref[:, cols])).astype(h_ref.dtype)


def _ffn_up(u, wg_all, wu_all, layer):
    m, k = u.shape
    n = wg_all.shape[2]
    tm, tn, cw = 2048, 512, 256
    wspec = pl.BlockSpec((None, k, tn), lambda j, i: (layer, 0, j))
    return pl.pallas_call(
        functools.partial(_ffn_up_kernel, cw=cw),
        grid=(n // tn, m // tm),
        in_specs=[pl.BlockSpec((tm, k), lambda j, i: (i, 0)), wspec, wspec],
        out_specs=pl.BlockSpec((tm, tn), lambda j, i: (i, j)),
        out_shape=jax.ShapeDtypeStruct((m, n), BF16),
        scratch_shapes=[pltpu.VMEM((k, tn), BF16), pltpu.VMEM((k, tn), BF16)],
        compiler_params=_cparams(("arbitrary", "arbitrary")),
        name="ffn_up",
    )(u, wg_all, wu_all)


def _down_kernel(h_ref, w_ref, x_ref, gate_ref, lng_ref, lnb_ref, sc_ref, sh_ref, *refs, sub, cw, emit_u):
    xo_ref = refs[0]
    uo_ref = refs[1] if emit_u else None
    z_ref = refs[-1]
    ychunk = lambda rows, cols: _dot(h_ref[rows, :], w_ref[:, cols])
    _matmul_residual_ln(ychunk, x_ref, gate_ref, lng_ref, lnb_ref, sc_ref, sh_ref, xo_ref, uo_ref, z_ref,
                        sub=sub, cw=cw)


def _down_ln(hid, w_all, layer, x2, gate, ln_g, ln_b, sc, sh, seq, emit_u):
    m, d = x2.shape
    kdim = hid.shape[1]
    tm, sub, cw = 256, 128, 512
    per_b = seq // tm
    vec = pl.BlockSpec((1, 1, d), lambda i: (i // per_b, 0, 0))
    par = pl.BlockSpec((1, d), lambda i: (0, 0))
    rowblk = pl.BlockSpec((tm, d), lambda i: (i, 0))
    n_out = 2 if emit_u else 1
    return pl.pallas_call(
        functools.partial(_down_kernel, sub=sub, cw=cw, emit_u=emit_u),
        grid=(m // tm,),
        in_specs=[pl.BlockSpec((tm, kdim), lambda i: (i, 0)),
                  pl.BlockSpec((None,) + w_all.shape[1:], lambda i: (layer, 0, 0), pipeline_mode=pl.Buffered(1)),
                  rowblk, vec, par, par, vec, vec],
        out_specs=[rowblk, rowblk][:n_out],
        out_shape=[jax.ShapeDtypeStruct((m, d), F32), jax.ShapeDtypeStruct((m, d), BF16)][:n_out],
        scratch_shapes=[pltpu.VMEM((tm, d), F32)],
        compiler_params=_cparams(("arbitrary",)),
        name="ffn_down_ln",
    )(hid, w_all, x2, gate, ln_g.reshape(1, d), ln_b.reshape(1, d), sc, sh)


def _prep_w_in(w_in):
    g0 = 3 * 1024
    g1 = g0 + 2 * M_HEADS
    big = jnp.concatenate([w_in[:, :, :g0], w_in[:, :, g1:]], axis=2).astype(BF16)
    gates = w_in[:, :, g0:g1]
    wgc = jnp.pad(gates, ((0, 0), (0, 0), (0, N_GATE - 2 * M_HEADS))).astype(BF16)
    wgr = jnp.pad(jnp.swapaxes(gates, 1, 2), ((0, 0), (0, BF16_ROWS - 2 * M_HEADS), (0, 0))).astype(BF16)
    return big, wgc, wgr


def kernel(x, c, w_mod, b_mod, w_in, conv_w, conv_b, b_igate, b_fgate, mlstm_norm_w, hgrn_norm_w, lb_logits, w_out, ln1_g, ln1_b, w_gate, w_up, w_down, ln2_g, ln2_b):
    batch, seq, d = x.shape
    depth = w_mod.shape[0]
    m = batch * seq
    x2 = x.reshape(m, d)

    c_pad = jnp.pad(c, ((0, 8 - batch), (0, 0)))
    mod = _mod_all(c_pad, w_mod, b_mod)[:, :batch]
    mod = mod.reshape(depth, batch, N_MOD, 1, d)
    mvec = lambda l, k: mod[l, :, k]

    w_big, wgc, wgr = _prep_w_in(w_in)
    w_down_bf = w_down.astype(BF16)
    u = _modulate(x2, mvec(0, 1), mvec(0, 0), seq)
    for l in range(depth):
        proj = _matmul(u, w_big, l, F32, 2048, 1024, "in_proj")
        ym = _mlstm(proj, u, wgc, wgr, l, conv_w[l], conv_b[l], b_igate[l], b_fgate[l], mlstm_norm_w[l],
                    batch, seq)
        yh = _hgrn(proj, lb_logits, hgrn_norm_w[l], l, batch, seq)
        x2, u = _outproj_ln(ym, yh, w_out, l, x2, mvec(l, 2), ln1_g[l], ln1_b[l], mvec(l, 4), mvec(l, 3), seq)
        hid = _ffn_up(u, w_gate, w_up, l)
        last = l == depth - 1
        nl = l if last else l + 1
        outs = _down_ln(hid, w_down_bf, l, x2, mvec(l, 5), ln2_g[l], ln2_b[l],
                        mvec(nl, 1), mvec(nl, 0), seq, not last)
        x2 = outs[0]
        u = None if last else outs[1]
    return x2.reshape(batch, seq, d)
```

```python
import functools
import math

import numpy as np
import jax
import jax.numpy as jnp
from jax import lax
from jax.experimental import pallas as pl
from jax.experimental.pallas import tpu as pltpu

F32 = jnp.float32
BF16 = jnp.bfloat16

D_MODEL = 2048
DEPTH = 4
D_MLSTM = 1024
D_HGRN = 1024
M_HEADS = 4
M_DV = 256
M_DQK = 128
H_HEADS = 8
H_DH = 128
CONV_W = 4
D_FF = 5632
N_MOD = 6
EPS = 1e-5
ALPHA = (2 * DEPTH) ** 0.25
N_GATE = 128
SUBLANES = 8
BF16_ROWS = 16
CONV_HALO = SUBLANES
LOG2E = 1.4426950408889634

VMEM_LIMIT = 56 * 1024 * 1024

MLSTM_CHUNK = 256
HGRN_CHUNK = 128
HGRN_BLOCK = 256


def _cparams(sem):
    return pltpu.CompilerParams(dimension_semantics=sem, vmem_limit_bytes=VMEM_LIMIT)


def _sigmoid(x):
    return 1.0 / (1.0 + jnp.exp(-x))


def _log_sigmoid(x):
    return jnp.minimum(x, 0.0) - jnp.log(1.0 + jnp.exp(-jnp.abs(x)))


def _split3(x):
    hi = x.astype(BF16)
    r1 = x - hi.astype(F32)
    mid = r1.astype(BF16)
    lo = (r1 - mid.astype(F32)).astype(BF16)
    return hi, mid, lo


def _dot(a, b):
    return jnp.dot(a, b, preferred_element_type=F32)


def _dot_nt(a, b):
    return lax.dot_general(a, b, (((1,), (1,)), ((), ())), preferred_element_type=F32)


def _dot_tn(a, b):
    return lax.dot_general(a, b, (((0,), (0,)), ((), ())), preferred_element_type=F32)


def _mod_kernel(c_ref, w_ref, b_ref, o_ref):
    c = c_ref[...]
    ca = (c * _sigmoid(c)).astype(BF16)
    o_ref[0] = _dot(ca, w_ref[0].astype(BF16)) + b_ref[0]


def _mod_all(c_pad, w_mod, b_mod):
    depth, d, n = w_mod.shape
    tn = 1024
    rows = c_pad.shape[0]
    return pl.pallas_call(
        _mod_kernel,
        grid=(depth, n // tn),
        in_specs=[pl.BlockSpec((rows, d), lambda l, j: (0, 0)),
                  pl.BlockSpec((1, d, tn), lambda l, j: (l, 0, j)),
                  pl.BlockSpec((1, 1, tn), lambda l, j: (l, 0, j))],
        out_specs=pl.BlockSpec((1, rows, tn), lambda l, j: (l, 0, j)),
        out_shape=jax.ShapeDtypeStruct((depth, rows, n), F32),
        compiler_params=_cparams(("arbitrary", "arbitrary")),
        name="adaln_mod",
    )(c_pad, w_mod, b_mod.reshape(depth, 1, n))


def _modulate_kernel(x_ref, sc_ref, sh_ref, u_ref):
    u_ref[...] = (x_ref[...] * (1.0 + sc_ref[0]) + sh_ref[0]).astype(u_ref.dtype)


def _modulate(x2, sc, sh, seq):
    m, d = x2.shape
    tm = 1024
    per_b = seq // tm
    vec = pl.BlockSpec((1, 1, d), lambda i: (i // per_b, 0, 0))
    return pl.pallas_call(
        _modulate_kernel,
        grid=(m // tm,),
        in_specs=[pl.BlockSpec((tm, d), lambda i: (i, 0)), vec, vec],
        out_specs=pl.BlockSpec((tm, d), lambda i: (i, 0)),
        out_shape=jax.ShapeDtypeStruct((m, d), BF16),
        compiler_params=_cparams(("arbitrary",)),
        name="modulate0",
    )(x2, sc, sh)


def _mm_kernel(x_ref, w_ref, o_ref):
    o_ref[...] = _dot(x_ref[...], w_ref[...]).astype(o_ref.dtype)


def _matmul(x, w_all, layer, out_dtype, tm, tn, name):
    m, k = x.shape
    n = w_all.shape[2]
    return pl.pallas_call(
        _mm_kernel,
        grid=(n // tn, m // tm),
        in_specs=[pl.BlockSpec((tm, k), lambda j, i: (i, 0)),
                  pl.BlockSpec((None, k, tn), lambda j, i: (layer, 0, j))],
        out_specs=pl.BlockSpec((tm, tn), lambda j, i: (i, j)),
        out_shape=jax.ShapeDtypeStruct((m, n), out_dtype),
        compiler_params=_cparams(("arbitrary", "arbitrary")),
        name=name,
    )(x, w_all)


def _mlstm_kernel(qk_ref, v_ref, og_ref, u_ref, wgc_ref, wgr_ref, cw_ref, cb_ref, bc_ref, br_ref, nw_ref,
                  tri_ref, trit_ref, y_ref, xp_ref, c_ref, n_ref, m_ref, *, L):
    dqk_all = M_HEADS * M_DQK

    @pl.when(pl.program_id(1) == 0)
    def _():
        xp_ref[0:CONV_HALO, :] = jnp.zeros((CONV_HALO, 2 * dqk_all), F32)
        c_ref[...] = jnp.zeros(c_ref.shape, F32)
        n_ref[...] = jnp.zeros(n_ref.shape, F32)
        m_ref[...] = jnp.zeros(m_ref.shape, F32)

    x = qk_ref[...]
    xp_ref[CONV_HALO:CONV_HALO + L, :] = x
    y = cb_ref[...]
    for j in range(CONV_W):
        off = CONV_HALO - (CONV_W - 1) + j
        y = y + cw_ref[j:j + 1, :] * xp_ref[off:off + L, :]
    xp_ref[0:CONV_HALO, :] = x[L - CONV_HALO:L, :]
    qk = y * _sigmoid(y)

    u = u_ref[...]
    gc = _dot(u, wgc_ref[...]) + bc_ref[...]
    gr = _dot_nt(wgr_ref[...], u)[0:2 * M_HEADS, :] + br_ref[...]
    bcum_c = _dot(tri_ref[...], jnp.concatenate(_split3(_log_sigmoid(gc) * LOG2E), axis=0))
    bcum_r = _dot(jnp.concatenate(_split3(_log_sigmoid(gr) * LOG2E), axis=1), trit_ref[...])
    gc = gc * LOG2E
    gr = gr * LOG2E

    ti = lax.broadcasted_iota(jnp.int32, (L, L), 0)
    si = lax.broadcasted_iota(jnp.int32, (L, L), 1)
    causal = si <= ti

    for h in range(M_HEADS):
        q = qk[:, h * M_DQK:(h + 1) * M_DQK] * (M_DQK ** -0.5)
        k = qk[:, dqk_all + h * M_DQK: dqk_all + (h + 1) * M_DQK]
        qb = q.astype(BF16)
        kb = k.astype(BF16)
        vb = v_ref[:, h * M_DV:(h + 1) * M_DV].astype(BF16)
        bc = bcum_c[:, M_HEADS + h:M_HEADS + h + 1]
        ic = gc[:, h:h + 1]
        br = bcum_r[M_HEADS + h:M_HEADS + h + 1, :]
        ir = gr[h:h + 1, :]
        c_prev = c_ref[h]
        n_prev = n_ref[h:h + 1, :]
        m_prev = m_ref[h:h + 1, 0:1]

        log_d = jnp.where(causal, bc - br + ir, -jnp.inf)
        m_inter = bc + m_prev
        m_t = jnp.maximum(m_inter, jnp.max(log_d, axis=1, keepdims=True))
        dw = jnp.exp2(log_d - m_t)
        w_inter = jnp.exp2(m_inter - m_t)
        s = _dot_nt(qb, kb) * dw
        num = _dot(s.astype(BF16), vb) + w_inter * _dot(qb, c_prev.astype(BF16))
        den = jnp.sum(s, axis=1, keepdims=True) + w_inter * jnp.sum(q * n_prev, axis=1, keepdims=True)
        hh = num * (1.0 / jnp.maximum(jnp.abs(den), jnp.exp2(-m_t)))

        b_last = bc[L - 1:L, :]
        lwe = b_last - bc + ic
        m_new = jnp.maximum(b_last + m_prev, jnp.max(lwe, axis=0, keepdims=True))
        w_end = jnp.exp2(lwe - m_new)
        decay = jnp.exp2(b_last + m_prev - m_new)
        kw = k * w_end
        c_ref[h] = decay * c_prev + _dot_tn(kw.astype(BF16), vb)
        n_ref[h:h + 1, :] = decay * n_prev + jnp.sum(kw, axis=0, keepdims=True)
        m_ref[h:h + 1, :] = jnp.broadcast_to(m_new, (1, m_ref.shape[1]))

        hc = hh - jnp.mean(hh, axis=1, keepdims=True)
        hn = hc * lax.rsqrt(jnp.mean(hc * hc, axis=1, keepdims=True) + EPS)
        sl = slice(h * M_DV, (h + 1) * M_DV)
        y_ref[:, sl] = (hn * nw_ref[:, sl] * _sigmoid(og_ref[:, sl])).astype(y_ref.dtype)


def _tri_consts(L):
    tri = np.tril(np.ones((L, L), np.float32))
    tri3 = np.concatenate([tri, tri, tri], axis=1)
    trit3 = np.concatenate([tri.T, tri.T, tri.T], axis=0)
    return jnp.asarray(tri3, BF16), jnp.asarray(trit3, BF16)


def _mlstm(proj, u, wgc_all, wgr_all, layer, conv_w, conv_b, b_ig, b_fg, norm_w, batch, seq):
    L = MLSTM_CHUNK
    nc = seq // L
    wq = 2 * M_HEADS * M_DQK
    tri3, trit3 = _tri_consts(L)
    bias_c = jnp.zeros((1, N_GATE), F32).at[0, :M_HEADS].set(b_ig).at[0, M_HEADS:2 * M_HEADS].set(b_fg)
    bias_r = jnp.concatenate([b_ig, b_fg]).reshape(2 * M_HEADS, 1)
    d = u.shape[1]
    row = lambda b, c: (b * nc + c, 0)
    const = lambda b, c: (0, 0)
    return pl.pallas_call(
        functools.partial(_mlstm_kernel, L=L),
        grid=(batch, nc),
        in_specs=[pl.BlockSpec((L, wq), lambda b, c: (b * nc + c, 0)),
                  pl.BlockSpec((L, D_MLSTM), lambda b, c: (b * nc + c, 1)),
                  pl.BlockSpec((L, D_MLSTM), lambda b, c: (b * nc + c, 2)),
                  pl.BlockSpec((L, d), row),
                  pl.BlockSpec((None, d, N_GATE), lambda b, c: (layer, 0, 0)),
                  pl.BlockSpec((None, BF16_ROWS, d), lambda b, c: (layer, 0, 0)),
                  pl.BlockSpec((CONV_W, wq), const),
                  pl.BlockSpec((1, wq), const),
                  pl.BlockSpec((1, N_GATE), const),
                  pl.BlockSpec((2 * M_HEADS, 1), const),
                  pl.BlockSpec((1, D_MLSTM), const),
                  pl.BlockSpec((L, 3 * L), const),
                  pl.BlockSpec((3 * L, L), const)],
        out_specs=pl.BlockSpec((L, D_MLSTM), row),
        out_shape=jax.ShapeDtypeStruct((batch * seq, D_MLSTM), BF16),
        scratch_shapes=[pltpu.VMEM((CONV_HALO + L, wq), F32),
                        pltpu.VMEM((M_HEADS, M_DQK, M_DV), F32),
                        pltpu.VMEM((8, M_DQK), F32),
                        pltpu.VMEM((8, 128), F32)],
        compiler_params=_cparams(("arbitrary", "arbitrary")),
        name="mlstm",
    )(proj, proj, proj, u, wgc_all, wgr_all, conv_w, conv_b.reshape(1, wq), bias_c, bias_r,
      norm_w.reshape(1, D_MLSTM), tri3, trit3)


def _hgrn_chunk(q_ref, f_ref, i_ref, g_ref, lbl_ref, nw_ref, tri_ref, y_ref, st_ref, b_ref, *, L, layer):
    dh_all = H_HEADS * H_DH
    zero = lambda n: jnp.zeros((n, dh_all), F32)

    qf = q_ref[...]
    f = f_ref[...]
    qs = qf * _sigmoid(qf)
    e = jnp.exp(-jnp.abs(f))
    d = 1.0 + e
    log_f = jnp.minimum(f, 0.0) - jnp.log(d)
    kk = jnp.where(f >= 0.0, e, 1.0) / d
    if layer > 0:
        lg = lbl_ref[...]
        ex = jnp.exp(lg - jnp.max(lg, axis=0, keepdims=True))
        pr = ex / jnp.sum(ex, axis=0, keepdims=True)
        lb = pr[1:2, :]
        for j in range(2, layer + 1):
            lb = lb + pr[j:j + 1, :]
        c = jnp.log1p(-lb) + log_f
        a = jnp.log(lb)
        log_f = jnp.maximum(a, c) + jnp.log(1.0 + jnp.exp(-jnp.abs(a - c)))
        kk = (1.0 - lb) * kk

    lf2 = log_f * LOG2E
    b = _dot(tri_ref[...], jnp.concatenate(_split3(lf2), axis=0))
    b_ref[...] = b
    b_last = b[L - 1:L, :]
    qb = (qs * jnp.exp2(b)).astype(BF16)
    kd = (kk * jnp.exp2(b_last - b)).astype(BF16)
    dec = jnp.exp2(b_last)

    sub = lax.broadcasted_iota(jnp.int32, (SUBLANES, dh_all), 0)
    groups = range(L // SUBLANES)
    grp = lambda v, g: v[g * SUBLANES:(g + 1) * SUBLANES]
    odd = (sub & 1) != 0
    f2 = jnp.exp2(lf2)
    q_lv = [qs.astype(BF16),
            jnp.concatenate([jnp.where(odd, grp(qs, g) * grp(f2, g), 0.0) for g in groups], axis=0).astype(BF16)]
    k_lv = [kk.astype(BF16),
            jnp.concatenate([jnp.where(odd, 0.0, grp(kk, g)) for g in groups], axis=0).astype(BF16)]
    half = 2
    while half < L:
        qp, kp = [], []
        if half < SUBLANES:
            upper = (sub & half) != 0
            for g in groups:
                r = None
                for mblk in range(SUBLANES // (2 * half)):
                    cand = jnp.broadcast_to(b_ref[pl.ds(g * SUBLANES + mblk * 2 * half + half - 1, 1), :],
                                            (SUBLANES, dh_all))
                    r = cand if r is None else jnp.where(sub >= mblk * 2 * half, cand, r)
                fac = jnp.exp2(-jnp.abs(grp(b, g) - r))
                qp.append(jnp.where(upper, grp(qs, g) * fac, 0.0))
                kp.append(jnp.where(upper, 0.0, grp(kk, g) * fac))
        else:
            for j in range(L // (2 * half)):
                lo = slice(j * 2 * half, j * 2 * half + half)
                up = slice(j * 2 * half + half, (j + 1) * 2 * half)
                r = b_ref[pl.ds(j * 2 * half + half - 1, 1), :]
                kp += [kk[lo] * jnp.exp2(r - b[lo]), zero(half)]
                qp += [zero(half), qs[up] * jnp.exp2(b[up] - r)]
        q_lv.append(jnp.concatenate(qp, axis=0).astype(BF16))
        k_lv.append(jnp.concatenate(kp, axis=0).astype(BF16))
        half *= 2

    ti = lax.broadcasted_iota(jnp.int32, (L, L), 0)
    si = lax.broadcasted_iota(jnp.int32, (L, L), 1)
    tx = ti ^ si

    atts = []
    for h in range(H_HEADS):
        sl = slice(h * H_DH, (h + 1) * H_DH)
        att = _dot_nt(q_lv[-1][:, sl], k_lv[-1][:, sl])
        for lev in range(len(q_lv) - 2, -1, -1):
            att = jnp.where(tx < (1 << lev), _dot_nt(q_lv[lev][:, sl], k_lv[lev][:, sl]), att)
        atts.append(att.astype(BF16))

    for h in range(H_HEADS):
        sl = slice(h * H_DH, (h + 1) * H_DH)
        vb = i_ref[:, sl].astype(BF16)
        st = st_ref[h]
        o = _dot(atts[h], vb) + _dot_nt(qb[:, sl], st.astype(BF16))
        st_ref[h] = st * dec[:, sl] + _dot_tn(vb, kd[:, sl])
        on = o * lax.rsqrt(jnp.mean(o * o, axis=1, keepdims=True) + EPS)
        g = g_ref[:, sl]
        y_ref[:, sl] = (on * nw_ref[:, sl] * (g * _sigmoid(g))).astype(y_ref.dtype)


def _hgrn_kernel(q_ref, f_ref, i_ref, g_ref, lbl_ref, nw_ref, tri_ref, y_ref, st_ref, b_ref, *, L, layer):
    @pl.when(pl.program_id(1) == 0)
    def _():
        st_ref[...] = jnp.zeros(st_ref.shape, F32)

    for ck in range(q_ref.shape[0] // L):
        rows = pl.ds(ck * L, L)
        _hgrn_chunk(q_ref.at[rows], f_ref.at[rows], i_ref.at[rows], g_ref.at[rows], lbl_ref, nw_ref, tri_ref,
                    y_ref.at[rows], st_ref, b_ref.at[rows], L=L, layer=layer)


def _hgrn(proj, lb_logits, norm_w, layer, batch, seq):
    L = HGRN_CHUNK
    rows = HGRN_BLOCK
    nc = seq // rows
    tri3, _ = _tri_consts(L)
    const = lambda b, c: (0, 0)
    col = lambda j: pl.BlockSpec((rows, D_HGRN), lambda b, c: (b * nc + c, j))
    return pl.pallas_call(
        functools.partial(_hgrn_kernel, L=L, layer=layer),
        grid=(batch, nc),
        in_specs=[col(3), col(4), col(5), col(6),
                  pl.BlockSpec(lb_logits.shape, const),
                  pl.BlockSpec((1, D_HGRN), const),
                  pl.BlockSpec(tri3.shape, const)],
        out_specs=pl.BlockSpec((rows, D_HGRN), lambda b, c: (b * nc + c, 0)),
        out_shape=jax.ShapeDtypeStruct((batch * seq, D_HGRN), BF16),
        scratch_shapes=[pltpu.VMEM((H_HEADS, H_DH, H_DH), F32), pltpu.VMEM((rows, D_HGRN), F32)],
        compiler_params=_cparams(("arbitrary", "arbitrary")),
        name="hgrn2",
    )(proj, proj, proj, proj, lb_logits, norm_w.reshape(1, D_HGRN), tri3)


def _matmul_residual_ln(ychunk, x_ref, gate_ref, lng_ref, lnb_ref, sc_ref, sh_ref, xo_ref, uo_ref, z_ref,
                        *, sub, cw):
    tm, d = z_ref.shape
    nsub, nchunk = tm // sub, d // cw
    gp = (1.0 + gate_ref[0]) * (1.0 / ALPHA)
    lng, lnb = lng_ref[...], lnb_ref[...]
    if uo_ref is not None:
        ug = lng * (1.0 + sc_ref[0])
        ub = lnb * (1.0 + sc_ref[0]) + sh_ref[0]
    mu = None
    for s in range(nsub + 1):
        rows = pl.ds(s * sub, sub)
        prev = pl.ds((s - 1) * sub, sub)
        rsum = None
        for j in range(nchunk):
            cols = slice(j * cw, (j + 1) * cw)
            if s < nsub:
                z = x_ref[rows, cols] + gp[:, cols] * ychunk(rows, cols)
                z_ref[rows, cols] = z
                part = jnp.sum(z, axis=1, keepdims=True)
                rsum = part if rsum is None else rsum + part
            if s > 0:
                if j == 0:
                    zc = z_ref[prev, :] - mu
                    rstd = lax.rsqrt(jnp.mean(zc * zc, axis=1, keepdims=True) + EPS / (ALPHA * ALPHA))
                t = (z_ref[prev, cols] - mu) * rstd
                xo_ref[prev, cols] = t * lng[:, cols] + lnb[:, cols]
                if uo_ref is not None:
                    uo_ref[prev, cols] = (t * ug[:, cols] + ub[:, cols]).astype(uo_ref.dtype)
        if s < nsub:
            mu = rsum * (1.0 / d)


def _outproj_kernel(ym_ref, yh_ref, w_ref, x_ref, gate_ref, lng_ref, lnb_ref, sc_ref, sh_ref, xo_ref, uo_ref,
                    wb_ref, z_ref, *, sub, cw):
    @pl.when(pl.program_id(0) == 0)
    def _():
        wb_ref[...] = w_ref[...].astype(BF16)

    def ychunk(rows, cols):
        return (_dot(ym_ref[rows, :], wb_ref[0:D_MLSTM, cols]) +
                _dot(yh_ref[rows, :], wb_ref[D_MLSTM:D_MLSTM + D_HGRN, cols]))

    _matmul_residual_ln(ychunk, x_ref, gate_ref, lng_ref, lnb_ref, sc_ref, sh_ref, xo_ref, uo_ref, z_ref,
                        sub=sub, cw=cw)


def _outproj_ln(ym, yh, w_all, layer, x2, gate, ln_g, ln_b, sc, sh, seq):
    m, d = x2.shape
    wshape = w_all.shape[1:]
    tm, sub, cw = 512, 128, 512
    per_b = seq // tm
    vec = pl.BlockSpec((1, 1, d), lambda i: (i // per_b, 0, 0))
    par = pl.BlockSpec((1, d), lambda i: (0, 0))
    rowblk = lambda width: pl.BlockSpec((tm, width), lambda i: (i, 0))
    return pl.pallas_call(
        functools.partial(_outproj_kernel, sub=sub, cw=cw),
        grid=(m // tm,),
        in_specs=[rowblk(D_MLSTM), rowblk(D_HGRN),
                  pl.BlockSpec((None,) + wshape, lambda i: (layer, 0, 0), pipeline_mode=pl.Buffered(1)),
                  rowblk(d), vec, par, par, vec, vec],
        out_specs=[rowblk(d), rowblk(d)],
        out_shape=[jax.ShapeDtypeStruct((m, d), F32), jax.ShapeDtypeStruct((m, d), BF16)],
        scratch_shapes=[pltpu.VMEM(wshape, BF16), pltpu.VMEM((tm, d), F32)],
        compiler_params=_cparams(("arbitrary",)),
        name="outproj_ln",
    )(ym, yh, w_all, x2, gate, ln_g.reshape(1, d), ln_b.reshape(1, d), sc, sh)


def _ffn_up_kernel(u_ref, wg_ref, wu_ref, h_ref, wgb_ref, wub_ref, *, sub):
    @pl.when(pl.program_id(1) == 0)
    def _():
        wgb_ref[...] = wg_ref[...].astype(BF16)
        wub_ref[...] = wu_ref[...].astype(BF16)

    for r in range(h_ref.shape[0] // sub):
        rows = pl.ds(r * sub, sub)
        u = u_ref[rows, :]
        g = _dot(u, wgb_ref[...])
        h_ref[rows, :] = (g * _sigmoid(g) * _dot(u, wub_ref[...])).astype(h_ref.dtype)


def _ffn_up(u, wg_all, wu_all, layer):
    m, k = u.shape
    n = wg_all.shape[2]
    tm, tn, sub = 2048, 512, 512
    wspec = pl.BlockSpec((None, k, tn), lambda j, i: (layer, 0, j))
    return pl.pallas_call(
        functools.partial(_ffn_up_kernel, sub=sub),
        grid=(n // tn, m // tm),
        in_specs=[pl.BlockSpec((tm, k), lambda j, i: (i, 0)), wspec, wspec],
        out_specs=pl.BlockSpec((tm, tn), lambda j, i: (i, j)),
        out_shape=jax.ShapeDtypeStruct((m, n), BF16),
        scratch_shapes=[pltpu.VMEM((k, tn), BF16), pltpu.VMEM((k, tn), BF16)],
        compiler_params=_cparams(("arbitrary", "arbitrary")),
        name="ffn_up",
    )(u, wg_all, wu_all)


def _down_kernel(h_ref, w_ref, x_ref, gate_ref, lng_ref, lnb_ref, sc_ref, sh_ref, *refs, sub, cw, emit_u):
    xo_ref = refs[0]
    uo_ref = refs[1] if emit_u else None
    z_ref = refs[-1]
    ychunk = lambda rows, cols: _dot(h_ref[rows, :], w_ref[:, cols])
    _matmul_residual_ln(ychunk, x_ref, gate_ref, lng_ref, lnb_ref, sc_ref, sh_ref, xo_ref, uo_ref, z_ref,
                        sub=sub, cw=cw)


def _down_ln(hid, w_all, layer, x2, gate, ln_g, ln_b, sc, sh, seq, emit_u):
    m, d = x2.shape
    kdim = hid.shape[1]
    tm, sub, cw = 256, 128, 512
    per_b = seq // tm
    vec = pl.BlockSpec((1, 1, d), lambda i: (i // per_b, 0, 0))
    par = pl.BlockSpec((1, d), lambda i: (0, 0))
    rowblk = pl.BlockSpec((tm, d), lambda i: (i, 0))
    n_out = 2 if emit_u else 1
    return pl.pallas_call(
        functools.partial(_down_kernel, sub=sub, cw=cw, emit_u=emit_u),
        grid=(m // tm,),
        in_specs=[pl.BlockSpec((tm, kdim), lambda i: (i, 0)),
                  pl.BlockSpec((None,) + w_all.shape[1:], lambda i: (layer, 0, 0), pipeline_mode=pl.Buffered(1)),
                  rowblk, vec, par, par, vec, vec],
        out_specs=[rowblk, rowblk][:n_out],
        out_shape=[jax.ShapeDtypeStruct((m, d), F32), jax.ShapeDtypeStruct((m, d), BF16)][:n_out],
        scratch_shapes=[pltpu.VMEM((tm, d), F32)],
        compiler_params=_cparams(("arbitrary",)),
        name="ffn_down_ln",
    )(hid, w_all, x2, gate, ln_g.reshape(1, d), ln_b.reshape(1, d), sc, sh)


def _w_in_prep_kernel(wa_ref, wb_ref, o_ref, *, n_head_blocks, shift, sub):
    j = pl.program_id(1)
    width = wa_ref.shape[1] + wb_ref.shape[1]

    @pl.when(j < n_head_blocks)
    def _():
        o_ref[...] = wa_ref[...].astype(o_ref.dtype)

    @pl.when(j >= n_head_blocks)
    def _():
        for r in range(wa_ref.shape[0] // sub):
            rows = pl.ds(r * sub, sub)
            full = jnp.concatenate([wa_ref[rows, :], wb_ref[rows, :]], axis=1)
            o_ref[rows, :] = pltpu.roll(full, width - shift, axis=1)[:, :wa_ref.shape[1]].astype(o_ref.dtype)


def _prep_w_in(w_in):
    depth, d, _ = w_in.shape
    g0 = 3 * 1024
    ng = 2 * M_HEADS
    tn, lane = 1024, 128
    n_blocks = (w_in.shape[2] - ng) // tn
    big = pl.pallas_call(
        functools.partial(_w_in_prep_kernel, n_head_blocks=g0 // tn, shift=ng, sub=512),
        grid=(depth, n_blocks),
        in_specs=[pl.BlockSpec((None, d, tn), lambda l, j: (l, 0, j)),
                  pl.BlockSpec((None, d, lane), lambda l, j: (l, 0, (j + 1) * (tn // lane)))],
        out_specs=pl.BlockSpec((None, d, tn), lambda l, j: (l, 0, j)),
        out_shape=jax.ShapeDtypeStruct((depth, d, n_blocks * tn), BF16),
        compiler_params=_cparams(("arbitrary", "arbitrary")),
        name="w_in_prep",
    )(w_in, w_in)
    gates = w_in[:, :, g0:g0 + ng]
    wgc = jnp.pad(gates, ((0, 0), (0, 0), (0, N_GATE - ng))).astype(BF16)
    wgr = jnp.pad(jnp.swapaxes(gates, 1, 2), ((0, 0), (0, BF16_ROWS - ng), (0, 0))).astype(BF16)
    return big, wgc, wgr


def kernel(x, c, w_mod, b_mod, w_in, conv_w, conv_b, b_igate, b_fgate, mlstm_norm_w, hgrn_norm_w, lb_logits, w_out, ln1_g, ln1_b, w_gate, w_up, w_down, ln2_g, ln2_b):
    batch, seq, d = x.shape
    depth = w_mod.shape[0]
    m = batch * seq
    x2 = x.reshape(m, d)

    c_pad = jnp.pad(c, ((0, 8 - batch), (0, 0)))
    mod = _mod_all(c_pad, w_mod, b_mod)[:, :batch]
    mod = mod.reshape(depth, batch, N_MOD, 1, d)
    mvec = lambda l, k: mod[l, :, k]

    w_big, wgc, wgr = _prep_w_in(w_in)
    w_down_bf = w_down.astype(BF16)
    u = _modulate(x2, mvec(0, 1), mvec(0, 0), seq)
    for l in range(depth):
        proj = _matmul(u, w_big, l, F32, 2048, 1024, "in_proj")
        ym = _mlstm(proj, u, wgc, wgr, l, conv_w[l], conv_b[l], b_igate[l], b_fgate[l], mlstm_norm_w[l],
                    batch, seq)
        yh = _hgrn(proj, lb_logits, hgrn_norm_w[l], l, batch, seq)
        x2, u = _outproj_ln(ym, yh, w_out, l, x2, mvec(l, 2), ln1_g[l], ln1_b[l], mvec(l, 4), mvec(l, 3), seq)
        hid = _ffn_up(u, w_gate, w_up, l)
        last = l == depth - 1
        nl = l if last else l + 1
        outs = _down_ln(hid, w_down_bf, l, x2, mvec(l, 5), ln2_g[l], ln2_b[l],
                        mvec(nl, 1), mvec(nl, 0), seq, not last)
        x2 = outs[0]
        u = None if last else outs[1]
    return x2.reshape(batch, seq, d)
```

```python
import functools
import math

import numpy as np
import jax
import jax.numpy as jnp
from jax import lax
from jax.experimental import pallas as pl
from jax.experimental.pallas import tpu as pltpu

F32 = jnp.float32
BF16 = jnp.bfloat16

D_MODEL = 2048
DEPTH = 4
D_MLSTM = 1024
D_HGRN = 1024
M_HEADS = 4
M_DV = 256
M_DQK = 128
H_HEADS = 8
H_DH = 128
CONV_W = 4
D_FF = 5632
N_MOD = 6
EPS = 1e-5
ALPHA = (2 * DEPTH) ** 0.25
N_GATE = 128
SUBLANES = 8
BF16_ROWS = 16
CONV_HALO = SUBLANES
LOG2E = 1.4426950408889634

VMEM_LIMIT = 56 * 1024 * 1024

MLSTM_CHUNK = 256
HGRN_CHUNK = 128
HGRN_BLOCK = 512


def _cparams(sem):
    return pltpu.CompilerParams(dimension_semantics=sem, vmem_limit_bytes=VMEM_LIMIT)


def _sigmoid(x):
    return 1.0 / (1.0 + jnp.exp(-x))


def _log_sigmoid(x):
    return jnp.minimum(x, 0.0) - jnp.log(1.0 + jnp.exp(-jnp.abs(x)))


def _split3(x):
    hi = x.astype(BF16)
    r1 = x - hi.astype(F32)
    mid = r1.astype(BF16)
    lo = (r1 - mid.astype(F32)).astype(BF16)
    return hi, mid, lo


def _dot(a, b):
    return jnp.dot(a, b, preferred_element_type=F32)


def _dot_nt(a, b):
    return lax.dot_general(a, b, (((1,), (1,)), ((), ())), preferred_element_type=F32)


def _dot_tn(a, b):
    return lax.dot_general(a, b, (((0,), (0,)), ((), ())), preferred_element_type=F32)


def _mod_kernel(c_ref, w_ref, b_ref, o_ref):
    c = c_ref[...]
    ca = (c * _sigmoid(c)).astype(BF16)
    o_ref[0] = _dot(ca, w_ref[0].astype(BF16)) + b_ref[0]


def _mod_all(c_pad, w_mod, b_mod):
    depth, d, n = w_mod.shape
    tn = 1024
    rows = c_pad.shape[0]
    return pl.pallas_call(
        _mod_kernel,
        grid=(depth, n // tn),
        in_specs=[pl.BlockSpec((rows, d), lambda l, j: (0, 0)),
                  pl.BlockSpec((1, d, tn), lambda l, j: (l, 0, j)),
                  pl.BlockSpec((1, 1, tn), lambda l, j: (l, 0, j))],
        out_specs=pl.BlockSpec((1, rows, tn), lambda l, j: (l, 0, j)),
        out_shape=jax.ShapeDtypeStruct((depth, rows, n), F32),
        compiler_params=_cparams(("arbitrary", "arbitrary")),
        name="adaln_mod",
    )(c_pad, w_mod, b_mod.reshape(depth, 1, n))


def _modulate_kernel(x_ref, sc_ref, sh_ref, u_ref):
    u_ref[...] = (x_ref[...] * (1.0 + sc_ref[0]) + sh_ref[0]).astype(u_ref.dtype)


def _modulate(x2, sc, sh, seq):
    m, d = x2.shape
    tm = 1024
    per_b = seq // tm
    vec = pl.BlockSpec((1, 1, d), lambda i: (i // per_b, 0, 0))
    return pl.pallas_call(
        _modulate_kernel,
        grid=(m // tm,),
        in_specs=[pl.BlockSpec((tm, d), lambda i: (i, 0)), vec, vec],
        out_specs=pl.BlockSpec((tm, d), lambda i: (i, 0)),
        out_shape=jax.ShapeDtypeStruct((m, d), BF16),
        compiler_params=_cparams(("arbitrary",)),
        name="modulate0",
    )(x2, sc, sh)


def _mm_kernel(x_ref, w_ref, o_ref):
    o_ref[...] = _dot(x_ref[...], w_ref[...]).astype(o_ref.dtype)


def _matmul(x, w_all, layer, out_dtype, tm, tn, name):
    m, k = x.shape
    n = w_all.shape[2]
    return pl.pallas_call(
        _mm_kernel,
        grid=(n // tn, m // tm),
        in_specs=[pl.BlockSpec((tm, k), lambda j, i: (i, 0)),
                  pl.BlockSpec((None, k, tn), lambda j, i: (layer, 0, j))],
        out_specs=pl.BlockSpec((tm, tn), lambda j, i: (i, j)),
        out_shape=jax.ShapeDtypeStruct((m, n), out_dtype),
        compiler_params=_cparams(("arbitrary", "arbitrary")),
        name=name,
    )(x, w_all)


def _mlstm_kernel(qk_ref, v_ref, og_ref, u_ref, wgc_ref, wgr_ref, cw_ref, cb_ref, bc_ref, br_ref, nw_ref,
                  tri_ref, trit_ref, y_ref, xp_ref, c_ref, n_ref, m_ref, *, L):
    dqk_all = M_HEADS * M_DQK

    @pl.when(pl.program_id(1) == 0)
    def _():
        xp_ref[0:CONV_HALO, :] = jnp.zeros((CONV_HALO, 2 * dqk_all), F32)
        c_ref[...] = jnp.zeros(c_ref.shape, F32)
        n_ref[...] = jnp.zeros(n_ref.shape, F32)
        m_ref[...] = jnp.zeros(m_ref.shape, F32)

    x = qk_ref[...]
    xp_ref[CONV_HALO:CONV_HALO + L, :] = x
    y = cb_ref[...]
    for j in range(CONV_W):
        off = CONV_HALO - (CONV_W - 1) + j
        y = y + cw_ref[j:j + 1, :] * xp_ref[off:off + L, :]
    xp_ref[0:CONV_HALO, :] = x[L - CONV_HALO:L, :]
    qk = y * _sigmoid(y)

    u = u_ref[...]
    gc = _dot(u, wgc_ref[...]) + bc_ref[...]
    gr = _dot_nt(wgr_ref[...], u)[0:2 * M_HEADS, :] + br_ref[...]
    bcum_c = _dot(tri_ref[...], jnp.concatenate(_split3(_log_sigmoid(gc) * LOG2E), axis=0))
    bcum_r = _dot(jnp.concatenate(_split3(_log_sigmoid(gr) * LOG2E), axis=1), trit_ref[...])
    gc = gc * LOG2E
    gr = gr * LOG2E

    ti = lax.broadcasted_iota(jnp.int32, (L, L), 0)
    si = lax.broadcasted_iota(jnp.int32, (L, L), 1)
    causal = si <= ti

    for h in range(M_HEADS):
        q = qk[:, h * M_DQK:(h + 1) * M_DQK] * (M_DQK ** -0.5)
        k = qk[:, dqk_all + h * M_DQK: dqk_all + (h + 1) * M_DQK]
        qb = q.astype(BF16)
        kb = k.astype(BF16)
        vb = v_ref[:, h * M_DV:(h + 1) * M_DV].astype(BF16)
        bc = bcum_c[:, M_HEADS + h:M_HEADS + h + 1]
        ic = gc[:, h:h + 1]
        br = bcum_r[M_HEADS + h:M_HEADS + h + 1, :]
        ir = gr[h:h + 1, :]
        c_prev = c_ref[h]
        n_prev = n_ref[h:h + 1, :]
        m_prev = m_ref[h:h + 1, 0:1]

        log_d = jnp.where(causal, bc - br + ir, -jnp.inf)
        m_inter = bc + m_prev
        m_t = jnp.maximum(m_inter, jnp.max(log_d, axis=1, keepdims=True))
        dw = jnp.exp2(log_d - m_t)
        w_inter = jnp.exp2(m_inter - m_t)
        s = _dot_nt(qb, kb) * dw
        num = _dot(s.astype(BF16), vb) + w_inter * _dot(qb, c_prev.astype(BF16))
        den = jnp.sum(s, axis=1, keepdims=True) + w_inter * jnp.sum(q * n_prev, axis=1, keepdims=True)
        hh = num * (1.0 / jnp.maximum(jnp.abs(den), jnp.exp2(-m_t)))

        b_last = bc[L - 1:L, :]
        lwe = b_last - bc + ic
        m_new = jnp.maximum(b_last + m_prev, jnp.max(lwe, axis=0, keepdims=True))
        w_end = jnp.exp2(lwe - m_new)
        decay = jnp.exp2(b_last + m_prev - m_new)
        kw = k * w_end
        c_ref[h] = decay * c_prev + _dot_tn(kw.astype(BF16), vb)
        n_ref[h:h + 1, :] = decay * n_prev + jnp.sum(kw, axis=0, keepdims=True)
        m_ref[h:h + 1, :] = jnp.broadcast_to(m_new, (1, m_ref.shape[1]))

        hc = hh - jnp.mean(hh, axis=1, keepdims=True)
        hn = hc * lax.rsqrt(jnp.mean(hc * hc, axis=1, keepdims=True) + EPS)
        sl = slice(h * M_DV, (h + 1) * M_DV)
        y_ref[:, sl] = (hn * nw_ref[:, sl] * _sigmoid(og_ref[:, sl])).astype(y_ref.dtype)


def _tri_consts(L):
    tri = np.tril(np.ones((L, L), np.float32))
    tri3 = np.concatenate([tri, tri, tri], axis=1)
    trit3 = np.concatenate([tri.T, tri.T, tri.T], axis=0)
    return jnp.asarray(tri3, BF16), jnp.asarray(trit3, BF16)


def _mlstm(proj, u, wgc_all, wgr_all, layer, conv_w, conv_b, b_ig, b_fg, norm_w, batch, seq):
    L = MLSTM_CHUNK
    nc = seq // L
    wq = 2 * M_HEADS * M_DQK
    tri3, trit3 = _tri_consts(L)
    bias_c = jnp.zeros((1, N_GATE), F32).at[0, :M_HEADS].set(b_ig).at[0, M_HEADS:2 * M_HEADS].set(b_fg)
    bias_r = jnp.concatenate([b_ig, b_fg]).reshape(2 * M_HEADS, 1)
    d = u.shape[1]
    row = lambda b, c: (b * nc + c, 0)
    const = lambda b, c: (0, 0)
    return pl.pallas_call(
        functools.partial(_mlstm_kernel, L=L),
        grid=(batch, nc),
        in_specs=[pl.BlockSpec((L, wq), lambda b, c: (b * nc + c, 0)),
                  pl.BlockSpec((L, D_MLSTM), lambda b, c: (b * nc + c, 1)),
                  pl.BlockSpec((L, D_MLSTM), lambda b, c: (b * nc + c, 2)),
                  pl.BlockSpec((L, d), row),
                  pl.BlockSpec((None, d, N_GATE), lambda b, c: (layer, 0, 0)),
                  pl.BlockSpec((None, BF16_ROWS, d), lambda b, c: (layer, 0, 0)),
                  pl.BlockSpec((CONV_W, wq), const),
                  pl.BlockSpec((1, wq), const),
                  pl.BlockSpec((1, N_GATE), const),
                  pl.BlockSpec((2 * M_HEADS, 1), const),
                  pl.BlockSpec((1, D_MLSTM), const),
                  pl.BlockSpec((L, 3 * L), const),
                  pl.BlockSpec((3 * L, L), const)],
        out_specs=pl.BlockSpec((L, D_MLSTM), row),
        out_shape=jax.ShapeDtypeStruct((batch * seq, D_MLSTM), BF16),
        scratch_shapes=[pltpu.VMEM((CONV_HALO + L, wq), F32),
                        pltpu.VMEM((M_HEADS, M_DQK, M_DV), F32),
                        pltpu.VMEM((8, M_DQK), F32),
                        pltpu.VMEM((8, 128), F32)],
        compiler_params=_cparams(("arbitrary", "arbitrary")),
        name="mlstm",
    )(proj, proj, proj, u, wgc_all, wgr_all, conv_w, conv_b.reshape(1, wq), bias_c, bias_r,
      norm_w.reshape(1, D_MLSTM), tri3, trit3)


def _hgrn_chunk(q_ref, f_ref, i_ref, g_ref, lbl_ref, nw_ref, tri_ref, y_ref, st_ref, b_ref, *, L, layer):
    dh_all = H_HEADS * H_DH
    zero = lambda n: jnp.zeros((n, dh_all), F32)

    qf = q_ref[...]
    f = f_ref[...]
    qs = qf * _sigmoid(qf)
    e = jnp.exp(-jnp.abs(f))
    d = 1.0 + e
    log_f = jnp.minimum(f, 0.0) - jnp.log(d)
    kk = jnp.where(f >= 0.0, e, 1.0) / d
    if layer > 0:
        lg = lbl_ref[...]
        ex = jnp.exp(lg - jnp.max(lg, axis=0, keepdims=True))
        pr = ex / jnp.sum(ex, axis=0, keepdims=True)
        lb = pr[1:2, :]
        for j in range(2, layer + 1):
            lb = lb + pr[j:j + 1, :]
        c = jnp.log1p(-lb) + log_f
        a = jnp.log(lb)
        log_f = jnp.maximum(a, c) + jnp.log(1.0 + jnp.exp(-jnp.abs(a - c)))
        kk = (1.0 - lb) * kk

    lf2 = log_f * LOG2E
    b = _dot(tri_ref[...], jnp.concatenate(_split3(lf2), axis=0))
    b_ref[...] = b
    b_last = b[L - 1:L, :]
    qb = (qs * jnp.exp2(b)).astype(BF16)
    kd = (kk * jnp.exp2(b_last - b)).astype(BF16)
    dec = jnp.exp2(b_last)

    sub = lax.broadcasted_iota(jnp.int32, (SUBLANES, dh_all), 0)
    groups = range(L // SUBLANES)
    grp = lambda v, g: v[g * SUBLANES:(g + 1) * SUBLANES]
    odd = (sub & 1) != 0
    f2 = jnp.exp2(lf2)
    q_lv = [qs.astype(BF16),
            jnp.concatenate([jnp.where(odd, grp(qs, g) * grp(f2, g), 0.0) for g in groups], axis=0).astype(BF16)]
    k_lv = [kk.astype(BF16),
            jnp.concatenate([jnp.where(odd, 0.0, grp(kk, g)) for g in groups], axis=0).astype(BF16)]
    half = 2
    while half < L:
        qp, kp = [], []
        if half < SUBLANES:
            upper = (sub & half) != 0
            for g in groups:
                r = None
                for mblk in range(SUBLANES // (2 * half)):
                    cand = jnp.broadcast_to(b_ref[pl.ds(g * SUBLANES + mblk * 2 * half + half - 1, 1), :],
                                            (SUBLANES, dh_all))
                    r = cand if r is None else jnp.where(sub >= mblk * 2 * half, cand, r)
                fac = jnp.exp2(-jnp.abs(grp(b, g) - r))
                qp.append(jnp.where(upper, grp(qs, g) * fac, 0.0))
                kp.append(jnp.where(upper, 0.0, grp(kk, g) * fac))
        else:
            for j in range(L // (2 * half)):
                lo = slice(j * 2 * half, j * 2 * half + half)
                up = slice(j * 2 * half + half, (j + 1) * 2 * half)
                r = b_ref[pl.ds(j * 2 * half + half - 1, 1), :]
                kp += [kk[lo] * jnp.exp2(r - b[lo]), zero(half)]
                qp += [zero(half), qs[up] * jnp.exp2(b[up] - r)]
        q_lv.append(jnp.concatenate(qp, axis=0).astype(BF16))
        k_lv.append(jnp.concatenate(kp, axis=0).astype(BF16))
        half *= 2

    ti = lax.broadcasted_iota(jnp.int32, (L, L), 0)
    si = lax.broadcasted_iota(jnp.int32, (L, L), 1)
    tx = ti ^ si

    atts = []
    for h in range(H_HEADS):
        sl = slice(h * H_DH, (h + 1) * H_DH)
        att = _dot_nt(q_lv[-1][:, sl], k_lv[-1][:, sl])
        for lev in range(len(q_lv) - 2, -1, -1):
            att = jnp.where(tx < (1 << lev), _dot_nt(q_lv[lev][:, sl], k_lv[lev][:, sl]), att)
        atts.append(att.astype(BF16))

    for h in range(H_HEADS):
        sl = slice(h * H_DH, (h + 1) * H_DH)
        vb = i_ref[:, sl].astype(BF16)
        st = st_ref[h]
        o = _dot(atts[h], vb) + _dot_nt(qb[:, sl], st.astype(BF16))
        st_ref[h] = st * dec[:, sl] + _dot_tn(vb, kd[:, sl])
        on = o * lax.rsqrt(jnp.mean(o * o, axis=1, keepdims=True) + EPS)
        g = g_ref[:, sl]
        y_ref[:, sl] = (on * nw_ref[:, sl] * (g * _sigmoid(g))).astype(y_ref.dtype)


def _hgrn_kernel(q_ref, f_ref, i_ref, g_ref, lbl_ref, nw_ref, tri_ref, y_ref, st_ref, b_ref, *, L, layer):
    @pl.when(pl.program_id(1) == 0)
    def _():
        st_ref[...] = jnp.zeros(st_ref.shape, F32)

    for ck in range(q_ref.shape[0] // L):
        rows = pl.ds(ck * L, L)
        _hgrn_chunk(q_ref.at[rows], f_ref.at[rows], i_ref.at[rows], g_ref.at[rows], lbl_ref, nw_ref, tri_ref,
                    y_ref.at[rows], st_ref, b_ref.at[rows], L=L, layer=layer)


def _hgrn(proj, lb_logits, norm_w, layer, batch, seq):
    L = HGRN_CHUNK
    rows = HGRN_BLOCK
    nc = seq // rows
    tri3, _ = _tri_consts(L)
    const = lambda b, c: (0, 0)
    col = lambda j: pl.BlockSpec((rows, D_HGRN), lambda b, c: (b * nc + c, j))
    return pl.pallas_call(
        functools.partial(_hgrn_kernel, L=L, layer=layer),
        grid=(batch, nc),
        in_specs=[col(3), col(4), col(5), col(6),
                  pl.BlockSpec(lb_logits.shape, const),
                  pl.BlockSpec((1, D_HGRN), const),
                  pl.BlockSpec(tri3.shape, const)],
        out_specs=pl.BlockSpec((rows, D_HGRN), lambda b, c: (b * nc + c, 0)),
        out_shape=jax.ShapeDtypeStruct((batch * seq, D_HGRN), BF16),
        scratch_shapes=[pltpu.VMEM((H_HEADS, H_DH, H_DH), F32), pltpu.VMEM((rows, D_HGRN), F32)],
        compiler_params=_cparams(("arbitrary", "arbitrary")),
        name="hgrn2",
    )(proj, proj, proj, proj, lb_logits, norm_w.reshape(1, D_HGRN), tri3)


def _matmul_residual_ln(ychunk, x_ref, gate_ref, lng_ref, lnb_ref, sc_ref, sh_ref, xo_ref, uo_ref, z_ref,
                        *, sub, cw):
    tm, d = z_ref.shape
    nsub, nchunk = tm // sub, d // cw
    gp = (1.0 + gate_ref[0]) * (1.0 / ALPHA)
    lng, lnb = lng_ref[...], lnb_ref[...]
    if uo_ref is not None:
        ug = lng * (1.0 + sc_ref[0])
        ub = lnb * (1.0 + sc_ref[0]) + sh_ref[0]
    mu = None
    for s in range(nsub + 1):
        rows = pl.ds(s * sub, sub)
        prev = pl.ds((s - 1) * sub, sub)
        rsum = None
        for j in range(nchunk):
            cols = slice(j * cw, (j + 1) * cw)
            if s < nsub:
                z = x_ref[rows, cols] + gp[:, cols] * ychunk(rows, cols)
                z_ref[rows, cols] = z
                part = jnp.sum(z, axis=1, keepdims=True)
                rsum = part if rsum is None else rsum + part
            if s > 0:
                if j == 0:
                    zc = z_ref[prev, :] - mu
                    rstd = lax.rsqrt(jnp.mean(zc * zc, axis=1, keepdims=True) + EPS / (ALPHA * ALPHA))
                t = (z_ref[prev, cols] - mu) * rstd
                xo_ref[prev, cols] = t * lng[:, cols] + lnb[:, cols]
                if uo_ref is not None:
                    uo_ref[prev, cols] = (t * ug[:, cols] + ub[:, cols]).astype(uo_ref.dtype)
        if s < nsub:
            mu = rsum * (1.0 / d)


def _outproj_kernel(ym_ref, yh_ref, w_ref, x_ref, gate_ref, lng_ref, lnb_ref, sc_ref, sh_ref, xo_ref, uo_ref,
                    wb_ref, z_ref, *, sub, cw):
    @pl.when(pl.program_id(0) == 0)
    def _():
        wb_ref[...] = w_ref[...].astype(BF16)

    def ychunk(rows, cols):
        return (_dot(ym_ref[rows, :], wb_ref[0:D_MLSTM, cols]) +
                _dot(yh_ref[rows, :], wb_ref[D_MLSTM:D_MLSTM + D_HGRN, cols]))

    _matmul_residual_ln(ychunk, x_ref, gate_ref, lng_ref, lnb_ref, sc_ref, sh_ref, xo_ref, uo_ref, z_ref,
                        sub=sub, cw=cw)


def _outproj_ln(ym, yh, w_all, layer, x2, gate, ln_g, ln_b, sc, sh, seq):
    m, d = x2.shape
    wshape = w_all.shape[1:]
    tm, sub, cw = 512, 128, 512
    per_b = seq // tm
    vec = pl.BlockSpec((1, 1, d), lambda i: (i // per_b, 0, 0))
    par = pl.BlockSpec((1, d), lambda i: (0, 0))
    rowblk = lambda width: pl.BlockSpec((tm, width), lambda i: (i, 0))
    return pl.pallas_call(
        functools.partial(_outproj_kernel, sub=sub, cw=cw),
        grid=(m // tm,),
        in_specs=[rowblk(D_MLSTM), rowblk(D_HGRN),
                  pl.BlockSpec((None,) + wshape, lambda i: (layer, 0, 0), pipeline_mode=pl.Buffered(1)),
                  rowblk(d), vec, par, par, vec, vec],
        out_specs=[rowblk(d), rowblk(d)],
        out_shape=[jax.ShapeDtypeStruct((m, d), F32), jax.ShapeDtypeStruct((m, d), BF16)],
        scratch_shapes=[pltpu.VMEM(wshape, BF16), pltpu.VMEM((tm, d), F32)],
        compiler_params=_cparams(("arbitrary",)),
        name="outproj_ln",
    )(ym, yh, w_all, x2, gate, ln_g.reshape(1, d), ln_b.reshape(1, d), sc, sh)


def _ffn_up_kernel(u_ref, wg_ref, wu_ref, h_ref, wgb_ref, wub_ref, *, sub):
    @pl.when(pl.program_id(1) == 0)
    def _():
        wgb_ref[...] = wg_ref[...].astype(BF16)
        wub_ref[...] = wu_ref[...].astype(BF16)

    for r in range(h_ref.shape[0] // sub):
        rows = pl.ds(r * sub, sub)
        u = u_ref[rows, :]
        g = _dot(u, wgb_ref[...])
        h_ref[rows, :] = (g * _sigmoid(g) * _dot(u, wub_ref[...])).astype(h_ref.dtype)


def _ffn_up(u, wg_all, wu_all, layer):
    m, k = u.shape
    n = wg_all.shape[2]
    tm, tn, sub = 2048, 512, 256
    wspec = pl.BlockSpec((None, k, tn), lambda j, i: (layer, 0, j))
    return pl.pallas_call(
        functools.partial(_ffn_up_kernel, sub=sub),
        grid=(n // tn, m // tm),
        in_specs=[pl.BlockSpec((tm, k), lambda j, i: (i, 0)), wspec, wspec],
        out_specs=pl.BlockSpec((tm, tn), lambda j, i: (i, j)),
        out_shape=jax.ShapeDtypeStruct((m, n), BF16),
        scratch_shapes=[pltpu.VMEM((k, tn), BF16), pltpu.VMEM((k, tn), BF16)],
        compiler_params=_cparams(("arbitrary", "arbitrary")),
        name="ffn_up",
    )(u, wg_all, wu_all)


def _down_kernel(h_ref, w_ref, x_ref, gate_ref, lng_ref, lnb_ref, sc_ref, sh_ref, *refs, sub, cw, emit_u):
    xo_ref = refs[0]
    uo_ref = refs[1] if emit_u else None
    z_ref = refs[-1]
    ychunk = lambda rows, cols: _dot(h_ref[rows, :], w_ref[:, cols])
    _matmul_residual_ln(ychunk, x_ref, gate_ref, lng_ref, lnb_ref, sc_ref, sh_ref, xo_ref, uo_ref, z_ref,
                        sub=sub, cw=cw)


def _down_ln(hid, w_all, layer, x2, gate, ln_g, ln_b, sc, sh, seq, emit_u):
    m, d = x2.shape
    kdim = hid.shape[1]
    tm, sub, cw = 256, 128, 512
    per_b = seq // tm
    vec = pl.BlockSpec((1, 1, d), lambda i: (i // per_b, 0, 0))
    par = pl.BlockSpec((1, d), lambda i: (0, 0))
    rowblk = pl.BlockSpec((tm, d), lambda i: (i, 0))
    n_out = 2 if emit_u else 1
    return pl.pallas_call(
        functools.partial(_down_kernel, sub=sub, cw=cw, emit_u=emit_u),
        grid=(m // tm,),
        in_specs=[pl.BlockSpec((tm, kdim), lambda i: (i, 0)),
                  pl.BlockSpec((None,) + w_all.shape[1:], lambda i: (layer, 0, 0), pipeline_mode=pl.Buffered(1)),
                  rowblk, vec, par, par, vec, vec],
        out_specs=[rowblk, rowblk][:n_out],
        out_shape=[jax.ShapeDtypeStruct((m, d), F32), jax.ShapeDtypeStruct((m, d), BF16)][:n_out],
        scratch_shapes=[pltpu.VMEM((tm, d), F32)],
        compiler_params=_cparams(("arbitrary",)),
        name="ffn_down_ln",
    )(hid, w_all, x2, gate, ln_g.reshape(1, d), ln_b.reshape(1, d), sc, sh)


def _prep_w_in(w_in):
    g0 = 3 * 1024
    g1 = g0 + 2 * M_HEADS
    big = jnp.concatenate([w_in[:, :, :g0], w_in[:, :, g1:]], axis=2).astype(BF16)
    gates = w_in[:, :, g0:g1]
    wgc = jnp.pad(gates, ((0, 0), (0, 0), (0, N_GATE - 2 * M_HEADS))).astype(BF16)
    wgr = jnp.pad(jnp.swapaxes(gates, 1, 2), ((0, 0), (0, BF16_ROWS - 2 * M_HEADS), (0, 0))).astype(BF16)
    return big, wgc, wgr


def kernel(x, c, w_mod, b_mod, w_in, conv_w, conv_b, b_igate, b_fgate, mlstm_norm_w, hgrn_norm_w, lb_logits, w_out, ln1_g, ln1_b, w_gate, w_up, w_down, ln2_g, ln2_b):
    batch, seq, d = x.shape
    depth = w_mod.shape[0]
    m = batch * seq
    x2 = x.reshape(m, d)

    c_pad = jnp.pad(c, ((0, 8 - batch), (0, 0)))
    mod = _mod_all(c_pad, w_mod, b_mod)[:, :batch]
    mod = mod.reshape(depth, batch, N_MOD, 1, d)
    mvec = lambda l, k: mod[l, :, k]

    w_big, wgc, wgr = _prep_w_in(w_in)
    w_down_bf = w_down.astype(BF16)
    u = _modulate(x2, mvec(0, 1), mvec(0, 0), seq)
    for l in range(depth):
        proj = _matmul(u, w_big, l, F32, 2048, 1024, "in_proj")
        ym = _mlstm(proj, u, wgc, wgr, l, conv_w[l], conv_b[l], b_igate[l], b_fgate[l], mlstm_norm_w[l],
                    batch, seq)
        yh = _hgrn(proj, lb_logits, hgrn_norm_w[l], l, batch, seq)
        x2, u = _outproj_ln(ym, yh, w_out, l, x2, mvec(l, 2), ln1_g[l], ln1_b[l], mvec(l, 4), mvec(l, 3), seq)
        hid = _ffn_up(u, w_gate, w_up, l)
        last = l == depth - 1
        nl = l if last else l + 1
        outs = _down_ln(hid, w_down_bf, l, x2, mvec(l, 5), ln2_g[l], ln2_b[l],
                        mvec(nl, 1), mvec(nl, 0), seq, not last)
        x2 = outs[0]
        u = None if last else outs[1]
    return x2.reshape(batch, seq, d)
```

```python
import functools

import numpy as np
import jax
import jax.numpy as jnp
from jax import lax
from jax.experimental import pallas as pl
from jax.experimental.pallas import tpu as pltpu

F32 = jnp.float32
BF16 = jnp.bfloat16

DEPTH = 4
D_MLSTM = 1024
D_HGRN = 1024
M_HEADS = 4
M_DV = 256
M_DQK = 128
H_HEADS = 8
H_DH = 128
CONV_W = 4
N_MOD = 6
EPS = 1e-5
ALPHA = (2 * DEPTH) ** 0.25
N_GATE = 128
SUBLANES = 8
BF16_ROWS = 16
CONV_HALO = SUBLANES
LOG2E = 1.4426950408889634

VMEM_LIMIT = 56 * 1024 * 1024

MLSTM_CHUNK = 256
HGRN_CHUNK = 128
HGRN_BLOCK = 512

MOD_COLS = 2048
MODULATE_ROWS = 1024
IN_PROJ_TILE = (2048, 1024)
OUTPROJ_TILE = (512, 128, 512)
FFN_UP_TILE = (2048, 512, 256)
FFN_DOWN_TILE = (256, 128, 512)


def _cparams(sem):
    return pltpu.CompilerParams(dimension_semantics=sem, vmem_limit_bytes=VMEM_LIMIT)


def _sigmoid(x):
    return 1.0 / (1.0 + jnp.exp(-x))


def _log_sigmoid(x):
    return jnp.minimum(x, 0.0) - jnp.log(1.0 + jnp.exp(-jnp.abs(x)))


def _split3(x):
    hi = x.astype(BF16)
    r1 = x - hi.astype(F32)
    mid = r1.astype(BF16)
    lo = (r1 - mid.astype(F32)).astype(BF16)
    return hi, mid, lo


def _dot(a, b):
    return jnp.dot(a, b, preferred_element_type=F32)


def _dot_nt(a, b):
    return lax.dot_general(a, b, (((1,), (1,)), ((), ())), preferred_element_type=F32)


def _dot_tn(a, b):
    return lax.dot_general(a, b, (((0,), (0,)), ((), ())), preferred_element_type=F32)


def _mod_kernel(c_ref, w_ref, b_ref, o_ref):
    c = c_ref[...]
    ca = (c * _sigmoid(c)).astype(BF16)
    o_ref[0] = _dot(ca, w_ref[0].astype(BF16)) + b_ref[0]


def _mod_all(c_pad, w_mod, b_mod):
    depth, d, n = w_mod.shape
    tn = MOD_COLS
    rows = c_pad.shape[0]
    return pl.pallas_call(
        _mod_kernel,
        grid=(depth, n // tn),
        in_specs=[pl.BlockSpec((rows, d), lambda l, j: (0, 0)),
                  pl.BlockSpec((1, d, tn), lambda l, j: (l, 0, j)),
                  pl.BlockSpec((1, 1, tn), lambda l, j: (l, 0, j))],
        out_specs=pl.BlockSpec((1, rows, tn), lambda l, j: (l, 0, j)),
        out_shape=jax.ShapeDtypeStruct((depth, rows, n), F32),
        compiler_params=_cparams(("arbitrary", "arbitrary")),
        name="adaln_mod",
    )(c_pad, w_mod, b_mod.reshape(depth, 1, n))


def _modulate_kernel(x_ref, sc_ref, sh_ref, u_ref):
    u_ref[...] = (x_ref[...] * (1.0 + sc_ref[0]) + sh_ref[0]).astype(u_ref.dtype)


def _modulate(x2, sc, sh, seq):
    m, d = x2.shape
    tm = MODULATE_ROWS
    per_b = seq // tm
    vec = pl.BlockSpec((1, 1, d), lambda i: (i // per_b, 0, 0))
    return pl.pallas_call(
        _modulate_kernel,
        grid=(m // tm,),
        in_specs=[pl.BlockSpec((tm, d), lambda i: (i, 0)), vec, vec],
        out_specs=pl.BlockSpec((tm, d), lambda i: (i, 0)),
        out_shape=jax.ShapeDtypeStruct((m, d), BF16),
        compiler_params=_cparams(("arbitrary",)),
        name="modulate0",
    )(x2, sc, sh)


def _mm_kernel(x_ref, w_ref, o_ref):
    o_ref[...] = _dot(x_ref[...], w_ref[...]).astype(o_ref.dtype)


def _matmul(x, w_all, layer, out_dtype, tm, tn, name):
    m, k = x.shape
    n = w_all.shape[2]
    return pl.pallas_call(
        _mm_kernel,
        grid=(n // tn, m // tm),
        in_specs=[pl.BlockSpec((tm, k), lambda j, i: (i, 0)),
                  pl.BlockSpec((None, k, tn), lambda j, i: (layer, 0, j))],
        out_specs=pl.BlockSpec((tm, tn), lambda j, i: (i, j)),
        out_shape=jax.ShapeDtypeStruct((m, n), out_dtype),
        compiler_params=_cparams(("arbitrary", "arbitrary")),
        name=name,
    )(x, w_all)


def _mlstm_kernel(qk_ref, v_ref, og_ref, u_ref, wgc_ref, wgr_ref, cw_ref, cb_ref, bc_ref, br_ref, nw_ref,
                  tri_ref, trit_ref, y_ref, xp_ref, c_ref, n_ref, m_ref, *, L):
    dqk_all = M_HEADS * M_DQK

    @pl.when(pl.program_id(1) == 0)
    def _():
        xp_ref[0:CONV_HALO, :] = jnp.zeros((CONV_HALO, 2 * dqk_all), F32)
        c_ref[...] = jnp.zeros(c_ref.shape, F32)
        n_ref[...] = jnp.zeros(n_ref.shape, F32)
        m_ref[...] = jnp.zeros(m_ref.shape, F32)

    x = qk_ref[...]
    xp_ref[CONV_HALO:CONV_HALO + L, :] = x
    y = cb_ref[...]
    for j in range(CONV_W):
        off = CONV_HALO - (CONV_W - 1) + j
        y = y + cw_ref[j:j + 1, :] * xp_ref[off:off + L, :]
    xp_ref[0:CONV_HALO, :] = x[L - CONV_HALO:L, :]
    qk = y * _sigmoid(y)

    u = u_ref[...]
    gc = _dot(u, wgc_ref[...]) + bc_ref[...]
    gr = _dot_nt(wgr_ref[...], u)[0:2 * M_HEADS, :] + br_ref[...]
    bcum_c = _dot(tri_ref[...], jnp.concatenate(_split3(_log_sigmoid(gc) * LOG2E), axis=0))
    bcum_r = _dot(jnp.concatenate(_split3(_log_sigmoid(gr) * LOG2E), axis=1), trit_ref[...])
    gc = gc * LOG2E
    gr = gr * LOG2E

    ti = lax.broadcasted_iota(jnp.int32, (L, L), 0)
    si = lax.broadcasted_iota(jnp.int32, (L, L), 1)
    causal = si <= ti

    for h in range(M_HEADS):
        q = qk[:, h * M_DQK:(h + 1) * M_DQK] * (M_DQK ** -0.5)
        k = qk[:, dqk_all + h * M_DQK: dqk_all + (h + 1) * M_DQK]
        qb = q.astype(BF16)
        kb = k.astype(BF16)
        vb = v_ref[:, h * M_DV:(h + 1) * M_DV].astype(BF16)
        bc = bcum_c[:, M_HEADS + h:M_HEADS + h + 1]
        ic = gc[:, h:h + 1]
        br = bcum_r[M_HEADS + h:M_HEADS + h + 1, :]
        ir = gr[h:h + 1, :]
        c_prev = c_ref[h]
        n_prev = n_ref[h:h + 1, :]
        m_prev = m_ref[h:h + 1, 0:1]

        log_d = jnp.where(causal, bc - br + ir, -jnp.inf)
        m_inter = bc + m_prev
        m_t = jnp.maximum(m_inter, jnp.max(log_d, axis=1, keepdims=True))
        dw = jnp.exp2(log_d - m_t)
        w_inter = jnp.exp2(m_inter - m_t)
        s = _dot_nt(qb, kb) * dw
        num = _dot(s.astype(BF16), vb) + w_inter * _dot(qb, c_prev.astype(BF16))
        den = jnp.sum(s, axis=1, keepdims=True) + w_inter * jnp.sum(q * n_prev, axis=1, keepdims=True)
        hh = num * (1.0 / jnp.maximum(jnp.abs(den), jnp.exp2(-m_t)))

        b_last = bc[L - 1:L, :]
        lwe = b_last - bc + ic
        m_new = jnp.maximum(b_last + m_prev, jnp.max(lwe, axis=0, keepdims=True))
        w_end = jnp.exp2(lwe - m_new)
        decay = jnp.exp2(b_last + m_prev - m_new)
        kw = k * w_end
        c_ref[h] = decay * c_prev + _dot_tn(kw.astype(BF16), vb)
        n_ref[h:h + 1, :] = decay * n_prev + jnp.sum(kw, axis=0, keepdims=True)
        m_ref[h:h + 1, :] = jnp.broadcast_to(m_new, (1, m_ref.shape[1]))

        hc = hh - jnp.mean(hh, axis=1, keepdims=True)
        hn = hc * lax.rsqrt(jnp.mean(hc * hc, axis=1, keepdims=True) + EPS)
        sl = slice(h * M_DV, (h + 1) * M_DV)
        y_ref[:, sl] = (hn * nw_ref[:, sl] * _sigmoid(og_ref[:, sl])).astype(y_ref.dtype)


def _tri_consts(L):
    tri = np.tril(np.ones((L, L), np.float32))
    tri3 = np.concatenate([tri, tri, tri], axis=1)
    trit3 = np.concatenate([tri.T, tri.T, tri.T], axis=0)
    return jnp.asarray(tri3, BF16), jnp.asarray(trit3, BF16)


def _mlstm(proj, u, wgc_all, wgr_all, layer, conv_w, conv_b, b_ig, b_fg, norm_w, batch, seq):
    L = MLSTM_CHUNK
    nc = seq // L
    wq = 2 * M_HEADS * M_DQK
    tri3, trit3 = _tri_consts(L)
    bias_c = jnp.zeros((1, N_GATE), F32).at[0, :M_HEADS].set(b_ig).at[0, M_HEADS:2 * M_HEADS].set(b_fg)
    bias_r = jnp.concatenate([b_ig, b_fg]).reshape(2 * M_HEADS, 1)
    d = u.shape[1]
    row = lambda b, c: (b * nc + c, 0)
    const = lambda b, c: (0, 0)
    return pl.pallas_call(
        functools.partial(_mlstm_kernel, L=L),
        grid=(batch, nc),
        in_specs=[pl.BlockSpec((L, wq), lambda b, c: (b * nc + c, 0)),
                  pl.BlockSpec((L, D_MLSTM), lambda b, c: (b * nc + c, 1)),
                  pl.BlockSpec((L, D_MLSTM), lambda b, c: (b * nc + c, 2)),
                  pl.BlockSpec((L, d), row),
                  pl.BlockSpec((None, d, N_GATE), lambda b, c: (layer, 0, 0)),
                  pl.BlockSpec((None, BF16_ROWS, d), lambda b, c: (layer, 0, 0)),
                  pl.BlockSpec((CONV_W, wq), const),
                  pl.BlockSpec((1, wq), const),
                  pl.BlockSpec((1, N_GATE), const),
                  pl.BlockSpec((2 * M_HEADS, 1), const),
                  pl.BlockSpec((1, D_MLSTM), const),
                  pl.BlockSpec((L, 3 * L), const),
                  pl.BlockSpec((3 * L, L), const)],
        out_specs=pl.BlockSpec((L, D_MLSTM), row),
        out_shape=jax.ShapeDtypeStruct((batch * seq, D_MLSTM), BF16),
        scratch_shapes=[pltpu.VMEM((CONV_HALO + L, wq), F32),
                        pltpu.VMEM((M_HEADS, M_DQK, M_DV), F32),
                        pltpu.VMEM((8, M_DQK), F32),
                        pltpu.VMEM((8, 128), F32)],
        compiler_params=_cparams(("arbitrary", "arbitrary")),
        name="mlstm",
    )(proj, proj, proj, u, wgc_all, wgr_all, conv_w, conv_b.reshape(1, wq), bias_c, bias_r,
      norm_w.reshape(1, D_MLSTM), tri3, trit3)


def _hgrn_chunk(q_ref, f_ref, i_ref, g_ref, lbl_ref, nw_ref, tri_ref, y_ref, st_ref, b_ref, *, L, layer):
    dh_all = H_HEADS * H_DH
    zero = lambda n: jnp.zeros((n, dh_all), F32)

    qf = q_ref[...]
    f = f_ref[...]
    qs = qf * _sigmoid(qf)
    e = jnp.exp(-jnp.abs(f))
    d = 1.0 + e
    log_f = jnp.minimum(f, 0.0) - jnp.log(d)
    kk = jnp.where(f >= 0.0, e, 1.0) / d
    if layer > 0:
        lg = lbl_ref[...]
        ex = jnp.exp(lg - jnp.max(lg, axis=0, keepdims=True))
        pr = ex / jnp.sum(ex, axis=0, keepdims=True)
        lb = pr[1:2, :]
        for j in range(2, layer + 1):
            lb = lb + pr[j:j + 1, :]
        c = jnp.log1p(-lb) + log_f
        a = jnp.log(lb)
        log_f = jnp.maximum(a, c) + jnp.log(1.0 + jnp.exp(-jnp.abs(a - c)))
        kk = (1.0 - lb) * kk

    lf2 = log_f * LOG2E
    b = _dot(tri_ref[...], jnp.concatenate(_split3(lf2), axis=0))
    b_ref[...] = b
    b_last = b[L - 1:L, :]
    qb = (qs * jnp.exp2(b)).astype(BF16)
    kd = (kk * jnp.exp2(b_last - b)).astype(BF16)
    dec = jnp.exp2(b_last)

    sub = lax.broadcasted_iota(jnp.int32, (SUBLANES, dh_all), 0)
    groups = range(L // SUBLANES)
    grp = lambda v, g: v[g * SUBLANES:(g + 1) * SUBLANES]
    odd = (sub & 1) != 0
    f2 = jnp.exp2(lf2)
    q_lv = [qs.astype(BF16),
            jnp.concatenate([jnp.where(odd, grp(qs, g) * grp(f2, g), 0.0) for g in groups], axis=0).astype(BF16)]
    k_lv = [kk.astype(BF16),
            jnp.concatenate([jnp.where(odd, 0.0, grp(kk, g)) for g in groups], axis=0).astype(BF16)]
    half = 2
    while half < L:
        qp, kp = [], []
        if half < SUBLANES:
            upper = (sub & half) != 0
            for g in groups:
                r = None
                for mblk in range(SUBLANES // (2 * half)):
                    cand = jnp.broadcast_to(b_ref[pl.ds(g * SUBLANES + mblk * 2 * half + half - 1, 1), :],
                                            (SUBLANES, dh_all))
                    r = cand if r is None else jnp.where(sub >= mblk * 2 * half, cand, r)
                fac = jnp.exp2(-jnp.abs(grp(b, g) - r))
                qp.append(jnp.where(upper, grp(qs, g) * fac, 0.0))
                kp.append(jnp.where(upper, 0.0, grp(kk, g) * fac))
        else:
            for j in range(L // (2 * half)):
                lo = slice(j * 2 * half, j * 2 * half + half)
                up = slice(j * 2 * half + half, (j + 1) * 2 * half)
                r = b_ref[pl.ds(j * 2 * half + half - 1, 1), :]
                kp += [kk[lo] * jnp.exp2(r - b[lo]), zero(half)]
                qp += [zero(half), qs[up] * jnp.exp2(b[up] - r)]
        q_lv.append(jnp.concatenate(qp, axis=0).astype(BF16))
        k_lv.append(jnp.concatenate(kp, axis=0).astype(BF16))
        half *= 2

    ti = lax.broadcasted_iota(jnp.int32, (L, L), 0)
    si = lax.broadcasted_iota(jnp.int32, (L, L), 1)
    txb = (ti ^ si).astype(F32).astype(BF16)

    atts = []
    for h in range(H_HEADS):
        sl = slice(h * H_DH, (h + 1) * H_DH)
        att = _dot_nt(q_lv[-1][:, sl], k_lv[-1][:, sl]).astype(BF16)
        for lev in range(len(q_lv) - 2, -1, -1):
            term = _dot_nt(q_lv[lev][:, sl], k_lv[lev][:, sl]).astype(BF16)
            att = jnp.where(txb < jnp.asarray(1 << lev, BF16), term, att)
        atts.append(att)

    for h in range(H_HEADS):
        sl = slice(h * H_DH, (h + 1) * H_DH)
        vb = i_ref[:, sl].astype(BF16)
        st = st_ref[h]
        o = _dot(atts[h], vb) + _dot_nt(qb[:, sl], st.astype(BF16))
        st_ref[h] = st * dec[:, sl] + _dot_tn(vb, kd[:, sl])
        on = o * lax.rsqrt(jnp.mean(o * o, axis=1, keepdims=True) + EPS)
        g = g_ref[:, sl]
        y_ref[:, sl] = (on * nw_ref[:, sl] * (g * _sigmoid(g))).astype(y_ref.dtype)


def _hgrn_kernel(q_ref, f_ref, i_ref, g_ref, lbl_ref, nw_ref, tri_ref, y_ref, st_ref, b_ref, *, L, layer):
    @pl.when(pl.program_id(1) == 0)
    def _():
        st_ref[...] = jnp.zeros(st_ref.shape, F32)

    for ck in range(q_ref.shape[0] // L):
        rows = pl.ds(ck * L, L)
        _hgrn_chunk(q_ref.at[rows], f_ref.at[rows], i_ref.at[rows], g_ref.at[rows], lbl_ref, nw_ref, tri_ref,
                    y_ref.at[rows], st_ref, b_ref.at[rows], L=L, layer=layer)


def _hgrn(proj, lb_logits, norm_w, layer, batch, seq):
    L = HGRN_CHUNK
    rows = HGRN_BLOCK
    nc = seq // rows
    tri3, _ = _tri_consts(L)
    const = lambda b, c: (0, 0)
    col = lambda j: pl.BlockSpec((rows, D_HGRN), lambda b, c: (b * nc + c, j))
    return pl.pallas_call(
        functools.partial(_hgrn_kernel, L=L, layer=layer),
        grid=(batch, nc),
        in_specs=[col(3), col(4), col(5), col(6),
                  pl.BlockSpec(lb_logits.shape, const),
                  pl.BlockSpec((1, D_HGRN), const),
                  pl.BlockSpec(tri3.shape, const)],
        out_specs=pl.BlockSpec((rows, D_HGRN), lambda b, c: (b * nc + c, 0)),
        out_shape=jax.ShapeDtypeStruct((batch * seq, D_HGRN), BF16),
        scratch_shapes=[pltpu.VMEM((H_HEADS, H_DH, H_DH), F32), pltpu.VMEM((rows, D_HGRN), F32)],
        compiler_params=_cparams(("arbitrary", "arbitrary")),
        name="hgrn2",
    )(proj, proj, proj, proj, lb_logits, norm_w.reshape(1, D_HGRN), tri3)


def _matmul_residual_ln(ychunk, x_ref, gate_ref, lng_ref, lnb_ref, sc_ref, sh_ref, xo_ref, uo_ref, z_ref,
                        *, sub, cw):
    tm, d = z_ref.shape
    nsub, nchunk = tm // sub, d // cw
    gp = (1.0 + gate_ref[0]) * (1.0 / ALPHA)
    lng, lnb = lng_ref[...], lnb_ref[...]
    if uo_ref is not None:
        ug = lng * (1.0 + sc_ref[0])
        ub = lnb * (1.0 + sc_ref[0]) + sh_ref[0]
    mu = None
    for s in range(nsub + 1):
        rows = pl.ds(s * sub, sub)
        prev = pl.ds((s - 1) * sub, sub)
        rsum = None
        for j in range(nchunk):
            cols = slice(j * cw, (j + 1) * cw)
            if s < nsub:
                z = x_ref[rows, cols] + gp[:, cols] * ychunk(rows, cols)
                z_ref[rows, cols] = z
                part = jnp.sum(z, axis=1, keepdims=True)
                rsum = part if rsum is None else rsum + part
            if s > 0:
                if j == 0:
                    zc = z_ref[prev, :] - mu
                    rstd = lax.rsqrt(jnp.mean(zc * zc, axis=1, keepdims=True) + EPS / (ALPHA * ALPHA))
                t = (z_ref[prev, cols] - mu) * rstd
                xo_ref[prev, cols] = t * lng[:, cols] + lnb[:, cols]
                if uo_ref is not None:
                    uo_ref[prev, cols] = (t * ug[:, cols] + ub[:, cols]).astype(uo_ref.dtype)
        if s < nsub:
            mu = rsum * (1.0 / d)


def _outproj_kernel(ym_ref, yh_ref, w_ref, x_ref, gate_ref, lng_ref, lnb_ref, sc_ref, sh_ref, xo_ref, uo_ref,
                    wb_ref, z_ref, *, sub, cw):
    @pl.when(pl.program_id(0) == 0)
    def _():
        wb_ref[...] = w_ref[...].astype(BF16)

    def ychunk(rows, cols):
        return (_dot(ym_ref[rows, :], wb_ref[0:D_MLSTM, cols]) +
                _dot(yh_ref[rows, :], wb_ref[D_MLSTM:D_MLSTM + D_HGRN, cols]))

    _matmul_residual_ln(ychunk, x_ref, gate_ref, lng_ref, lnb_ref, sc_ref, sh_ref, xo_ref, uo_ref, z_ref,
                        sub=sub, cw=cw)


def _outproj_ln(ym, yh, w_all, layer, x2, gate, ln_g, ln_b, sc, sh, seq):
    m, d = x2.shape
    wshape = w_all.shape[1:]
    tm, sub, cw = OUTPROJ_TILE
    per_b = seq // tm
    vec = pl.BlockSpec((1, 1, d), lambda i: (i // per_b, 0, 0))
    par = pl.BlockSpec((1, d), lambda i: (0, 0))
    rowblk = lambda width: pl.BlockSpec((tm, width), lambda i: (i, 0))
    return pl.pallas_call(
        functools.partial(_outproj_kernel, sub=sub, cw=cw),
        grid=(m // tm,),
        in_specs=[rowblk(D_MLSTM), rowblk(D_HGRN),
                  pl.BlockSpec((None,) + wshape, lambda i: (layer, 0, 0), pipeline_mode=pl.Buffered(1)),
                  rowblk(d), vec, par, par, vec, vec],
        out_specs=[rowblk(d), rowblk(d)],
        out_shape=[jax.ShapeDtypeStruct((m, d), F32), jax.ShapeDtypeStruct((m, d), BF16)],
        scratch_shapes=[pltpu.VMEM(wshape, BF16), pltpu.VMEM((tm, d), F32)],
        compiler_params=_cparams(("arbitrary",)),
        name="outproj_ln",
    )(ym, yh, w_all, x2, gate, ln_g.reshape(1, d), ln_b.reshape(1, d), sc, sh)


def _ffn_up_kernel(u_ref, wg_ref, wu_ref, h_ref, wgb_ref, wub_ref, *, sub):
    @pl.when(pl.program_id(1) == 0)
    def _():
        wgb_ref[...] = wg_ref[...].astype(BF16)
        wub_ref[...] = wu_ref[...].astype(BF16)

    for r in range(h_ref.shape[0] // sub):
        rows = pl.ds(r * sub, sub)
        u = u_ref[rows, :]
        g = _dot(u, wgb_ref[...])
        h_ref[rows, :] = (g * _sigmoid(g) * _dot(u, wub_ref[...])).astype(h_ref.dtype)


def _ffn_up(u, wg_all, wu_all, layer):
    m, k = u.shape
    n = wg_all.shape[2]
    tm, tn, sub = FFN_UP_TILE
    wspec = pl.BlockSpec((None, k, tn), lambda j, i: (layer, 0, j))
    return pl.pallas_call(
        functools.partial(_ffn_up_kernel, sub=sub),
        grid=(n // tn, m // tm),
        in_specs=[pl.BlockSpec((tm, k), lambda j, i: (i, 0)), wspec, wspec],
        out_specs=pl.BlockSpec((tm, tn), lambda j, i: (i, j)),
        out_shape=jax.ShapeDtypeStruct((m, n), BF16),
        scratch_shapes=[pltpu.VMEM((k, tn), BF16), pltpu.VMEM((k, tn), BF16)],
        compiler_params=_cparams(("arbitrary", "arbitrary")),
        name="ffn_up",
    )(u, wg_all, wu_all)


def _down_kernel(h_ref, w_ref, x_ref, gate_ref, lng_ref, lnb_ref, sc_ref, sh_ref, *refs, sub, cw, emit_u):
    xo_ref = refs[0]
    uo_ref = refs[1] if emit_u else None
    z_ref = refs[-1]
    ychunk = lambda rows, cols: _dot(h_ref[rows, :], w_ref[:, cols])
    _matmul_residual_ln(ychunk, x_ref, gate_ref, lng_ref, lnb_ref, sc_ref, sh_ref, xo_ref, uo_ref, z_ref,
                        sub=sub, cw=cw)


def _down_ln(hid, w_all, layer, x2, gate, ln_g, ln_b, sc, sh, seq, emit_u):
    m, d = x2.shape
    kdim = hid.shape[1]
    tm, sub, cw = FFN_DOWN_TILE
    per_b = seq // tm
    vec = pl.BlockSpec((1, 1, d), lambda i: (i // per_b, 0, 0))
    par = pl.BlockSpec((1, d), lambda i: (0, 0))
    rowblk = pl.BlockSpec((tm, d), lambda i: (i, 0))
    n_out = 2 if emit_u else 1
    return pl.pallas_call(
        functools.partial(_down_kernel, sub=sub, cw=cw, emit_u=emit_u),
        grid=(m // tm,),
        in_specs=[pl.BlockSpec((tm, kdim), lambda i: (i, 0)),
                  pl.BlockSpec((None,) + w_all.shape[1:], lambda i: (layer, 0, 0), pipeline_mode=pl.Buffered(1)),
                  rowblk, vec, par, par, vec, vec],
        out_specs=[rowblk, rowblk][:n_out],
        out_shape=[jax.ShapeDtypeStruct((m, d), F32), jax.ShapeDtypeStruct((m, d), BF16)][:n_out],
        scratch_shapes=[pltpu.VMEM((tm, d), F32)],
        compiler_params=_cparams(("arbitrary",)),
        name="ffn_down_ln",
    )(hid, w_all, x2, gate, ln_g.reshape(1, d), ln_b.reshape(1, d), sc, sh)


def _prep_w_in(w_in):
    g0 = 3 * 1024
    g1 = g0 + 2 * M_HEADS
    big = jnp.concatenate([w_in[:, :, :g0], w_in[:, :, g1:]], axis=2).astype(BF16)
    gates = w_in[:, :, g0:g1]
    wgc = jnp.pad(gates, ((0, 0), (0, 0), (0, N_GATE - 2 * M_HEADS))).astype(BF16)
    wgr = jnp.pad(jnp.swapaxes(gates, 1, 2), ((0, 0), (0, BF16_ROWS - 2 * M_HEADS), (0, 0))).astype(BF16)
    return big, wgc, wgr


def kernel(x, c, w_mod, b_mod, w_in, conv_w, conv_b, b_igate, b_fgate, mlstm_norm_w, hgrn_norm_w, lb_logits, w_out, ln1_g, ln1_b, w_gate, w_up, w_down, ln2_g, ln2_b):
    batch, seq, d = x.shape
    depth = w_mod.shape[0]
    m = batch * seq
    x2 = x.reshape(m, d)

    c_pad = jnp.pad(c, ((0, 8 - batch), (0, 0)))
    mod = _mod_all(c_pad, w_mod, b_mod)[:, :batch]
    mod = mod.reshape(depth, batch, N_MOD, 1, d)
    mvec = lambda l, k: mod[l, :, k]

    w_big, wgc, wgr = _prep_w_in(w_in)
    w_down_bf = w_down.astype(BF16)
    u = _modulate(x2, mvec(0, 1), mvec(0, 0), seq)
    for l in range(depth):
        proj = _matmul(u, w_big, l, F32, *IN_PROJ_TILE, "in_proj")
        ym = _mlstm(proj, u, wgc, wgr, l, conv_w[l], conv_b[l], b_igate[l], b_fgate[l], mlstm_norm_w[l],
                    batch, seq)
        yh = _hgrn(proj, lb_logits, hgrn_norm_w[l], l, batch, seq)
        x2, u = _outproj_ln(ym, yh, w_out, l, x2, mvec(l, 2), ln1_g[l], ln1_b[l], mvec(l, 4), mvec(l, 3), seq)
        hid = _ffn_up(u, w_gate, w_up, l)
        last = l == depth - 1
        nl = l if last else l + 1
        outs = _down_ln(hid, w_down_bf, l, x2, mvec(l, 5), ln2_g[l], ln2_b[l],
                        mvec(nl, 1), mvec(nl, 0), seq, not last)
        x2 = outs[0]
        u = None if last else outs[1]
    return x2.reshape(batch, seq, d)
```

```python
import functools

import numpy as np
import jax
import jax.numpy as jnp
from jax import lax
from jax.experimental import pallas as pl
from jax.experimental.pallas import tpu as pltpu

F32 = jnp.float32
BF16 = jnp.bfloat16

DEPTH = 4
D_MLSTM = 1024
D_HGRN = 1024
M_HEADS = 4
M_DV = 256
M_DQK = 128
H_HEADS = 8
H_DH = 128
CONV_W = 4
N_MOD = 6
EPS = 1e-5
ALPHA = (2 * DEPTH) ** 0.25
N_GATE = 128
SUBLANES = 8
BF16_ROWS = 16
CONV_HALO = SUBLANES
LOG2E = 1.4426950408889634

VMEM_LIMIT = 58 * 1024 * 1024

MLSTM_CHUNK = 256
HGRN_CHUNK = 128
HGRN_BLOCK = 512

MOD_COLS = 2048
MODULATE_ROWS = 1024
IN_PROJ_TILE = (2048, 1024)
OUTPROJ_TILE = (512, 128, 512)
FFN_UP_TILE = (2048, 512, 256)
FFN_DOWN_TILE = (512, 128, 512)


def _cparams(sem):
    return pltpu.CompilerParams(dimension_semantics=sem, vmem_limit_bytes=VMEM_LIMIT)


def _sigmoid(x):
    return 1.0 / (1.0 + jnp.exp(-x))


def _log_sigmoid(x):
    return jnp.minimum(x, 0.0) - jnp.log(1.0 + jnp.exp(-jnp.abs(x)))


def _split3(x):
    hi = x.astype(BF16)
    r1 = x - hi.astype(F32)
    mid = r1.astype(BF16)
    lo = (r1 - mid.astype(F32)).astype(BF16)
    return hi, mid, lo


def _dot(a, b):
    return jnp.dot(a, b, preferred_element_type=F32)


def _dot_nt(a, b):
    return lax.dot_general(a, b, (((1,), (1,)), ((), ())), preferred_element_type=F32)


def _dot_tn(a, b):
    return lax.dot_general(a, b, (((0,), (0,)), ((), ())), preferred_element_type=F32)


def _mod_kernel(c_ref, w_ref, b_ref, o_ref):
    c = c_ref[...]
    ca = (c * _sigmoid(c)).astype(BF16)
    o_ref[0] = _dot(ca, w_ref[0].astype(BF16)) + b_ref[0]


def _mod_all(c_pad, w_mod, b_mod):
    depth, d, n = w_mod.shape
    tn = MOD_COLS
    rows = c_pad.shape[0]
    return pl.pallas_call(
        _mod_kernel,
        grid=(depth, n // tn),
        in_specs=[pl.BlockSpec((rows, d), lambda l, j: (0, 0)),
                  pl.BlockSpec((1, d, tn), lambda l, j: (l, 0, j)),
                  pl.BlockSpec((1, 1, tn), lambda l, j: (l, 0, j))],
        out_specs=pl.BlockSpec((1, rows, tn), lambda l, j: (l, 0, j)),
        out_shape=jax.ShapeDtypeStruct((depth, rows, n), F32),
        compiler_params=_cparams(("arbitrary", "arbitrary")),
        name="adaln_mod",
    )(c_pad, w_mod, b_mod.reshape(depth, 1, n))


def _modulate_kernel(x_ref, sc_ref, sh_ref, u_ref):
    u_ref[...] = (x_ref[...] * (1.0 + sc_ref[0]) + sh_ref[0]).astype(u_ref.dtype)


def _modulate(x2, sc, sh, seq):
    m, d = x2.shape
    tm = MODULATE_ROWS
    per_b = seq // tm
    vec = pl.BlockSpec((1, 1, d), lambda i: (i // per_b, 0, 0))
    return pl.pallas_call(
        _modulate_kernel,
        grid=(m // tm,),
        in_specs=[pl.BlockSpec((tm, d), lambda i: (i, 0)), vec, vec],
        out_specs=pl.BlockSpec((tm, d), lambda i: (i, 0)),
        out_shape=jax.ShapeDtypeStruct((m, d), BF16),
        compiler_params=_cparams(("arbitrary",)),
        name="modulate0",
    )(x2, sc, sh)


def _mm_kernel(x_ref, w_ref, o_ref):
    o_ref[...] = _dot(x_ref[...], w_ref[...]).astype(o_ref.dtype)


def _matmul(x, w_all, layer, out_dtype, tm, tn, name):
    m, k = x.shape
    n = w_all.shape[2]
    return pl.pallas_call(
        _mm_kernel,
        grid=(n // tn, m // tm),
        in_specs=[pl.BlockSpec((tm, k), lambda j, i: (i, 0)),
                  pl.BlockSpec((None, k, tn), lambda j, i: (layer, 0, j))],
        out_specs=pl.BlockSpec((tm, tn), lambda j, i: (i, j)),
        out_shape=jax.ShapeDtypeStruct((m, n), out_dtype),
        compiler_params=_cparams(("arbitrary", "arbitrary")),
        name=name,
    )(x, w_all)


def _mlstm_kernel(qk_ref, v_ref, og_ref, u_ref, wgc_ref, wgr_ref, cw_ref, cb_ref, bc_ref, br_ref, nw_ref,
                  tri_ref, trit_ref, y_ref, xp_ref, c_ref, n_ref, m_ref, *, L):
    dqk_all = M_HEADS * M_DQK

    @pl.when(pl.program_id(1) == 0)
    def _():
        xp_ref[0:CONV_HALO, :] = jnp.zeros((CONV_HALO, 2 * dqk_all), F32)
        c_ref[...] = jnp.zeros(c_ref.shape, F32)
        n_ref[...] = jnp.zeros(n_ref.shape, F32)
        m_ref[...] = jnp.zeros(m_ref.shape, F32)

    x = qk_ref[...]
    xp_ref[CONV_HALO:CONV_HALO + L, :] = x
    y = cb_ref[...]
    for j in range(CONV_W):
        off = CONV_HALO - (CONV_W - 1) + j
        y = y + cw_ref[j:j + 1, :] * xp_ref[off:off + L, :]
    xp_ref[0:CONV_HALO, :] = x[L - CONV_HALO:L, :]
    qk = y * _sigmoid(y)

    u = u_ref[...]
    gc = _dot(u, wgc_ref[...]) + bc_ref[...]
    gr = _dot_nt(wgr_ref[...], u)[0:2 * M_HEADS, :] + br_ref[...]
    bcum_c = _dot(tri_ref[...], jnp.concatenate(_split3(_log_sigmoid(gc) * LOG2E), axis=0))
    bcum_r = _dot(jnp.concatenate(_split3(_log_sigmoid(gr) * LOG2E), axis=1), trit_ref[...])
    gc = gc * LOG2E
    gr = gr * LOG2E

    ti = lax.broadcasted_iota(jnp.int32, (L, L), 0)
    si = lax.broadcasted_iota(jnp.int32, (L, L), 1)
    causal = si <= ti

    for h in range(M_HEADS):
        q = qk[:, h * M_DQK:(h + 1) * M_DQK] * (M_DQK ** -0.5)
        k = qk[:, dqk_all + h * M_DQK: dqk_all + (h + 1) * M_DQK]
        qb = q.astype(BF16)
        kb = k.astype(BF16)
        vb = v_ref[:, h * M_DV:(h + 1) * M_DV].astype(BF16)
        bc = bcum_c[:, M_HEADS + h:M_HEADS + h + 1]
        ic = gc[:, h:h + 1]
        br = bcum_r[M_HEADS + h:M_HEADS + h + 1, :]
        ir = gr[h:h + 1, :]
        c_prev = c_ref[h]
        n_prev = n_ref[h:h + 1, :]
        m_prev = m_ref[h:h + 1, 0:1]

        log_d = jnp.where(causal, bc - br + ir, -jnp.inf)
        m_inter = bc + m_prev
        m_t = jnp.maximum(m_inter, jnp.max(log_d, axis=1, keepdims=True))
        dw = jnp.exp2(log_d - m_t)
        w_inter = jnp.exp2(m_inter - m_t)
        s = _dot_nt(qb, kb) * dw
        num = _dot(s.astype(BF16), vb) + w_inter * _dot(qb, c_prev.astype(BF16))
        den = jnp.sum(s, axis=1, keepdims=True) + w_inter * jnp.sum(q * n_prev, axis=1, keepdims=True)
        hh = num * (1.0 / jnp.maximum(jnp.abs(den), jnp.exp2(-m_t)))

        b_last = bc[L - 1:L, :]
        lwe = b_last - bc + ic
        m_new = jnp.maximum(b_last + m_prev, jnp.max(lwe, axis=0, keepdims=True))
        w_end = jnp.exp2(lwe - m_new)
        decay = jnp.exp2(b_last + m_prev - m_new)
        kw = k * w_end
        c_ref[h] = decay * c_prev + _dot_tn(kw.astype(BF16), vb)
        n_ref[h:h + 1, :] = decay * n_prev + jnp.sum(kw, axis=0, keepdims=True)
        m_ref[h:h + 1, :] = jnp.broadcast_to(m_new, (1, m_ref.shape[1]))

        hc = hh - jnp.mean(hh, axis=1, keepdims=True)
        hn = hc * lax.rsqrt(jnp.mean(hc * hc, axis=1, keepdims=True) + EPS)
        sl = slice(h * M_DV, (h + 1) * M_DV)
        y_ref[:, sl] = (hn * nw_ref[:, sl] * _sigmoid(og_ref[:, sl])).astype(y_ref.dtype)


def _tri_consts(L):
    tri = np.tril(np.ones((L, L), np.float32))
    tri3 = np.concatenate([tri, tri, tri], axis=1)
    trit3 = np.concatenate([tri.T, tri.T, tri.T], axis=0)
    return jnp.asarray(tri3, BF16), jnp.asarray(trit3, BF16)


def _mlstm(proj, u, wgc_all, wgr_all, layer, conv_w, conv_b, b_ig, b_fg, norm_w, batch, seq):
    L = MLSTM_CHUNK
    nc = seq // L
    wq = 2 * M_HEADS * M_DQK
    tri3, trit3 = _tri_consts(L)
    bias_c = jnp.zeros((1, N_GATE), F32).at[0, :M_HEADS].set(b_ig).at[0, M_HEADS:2 * M_HEADS].set(b_fg)
    bias_r = jnp.concatenate([b_ig, b_fg]).reshape(2 * M_HEADS, 1)
    d = u.shape[1]
    row = lambda b, c: (b * nc + c, 0)
    const = lambda b, c: (0, 0)
    return pl.pallas_call(
        functools.partial(_mlstm_kernel, L=L),
        grid=(batch, nc),
        in_specs=[pl.BlockSpec((L, wq), lambda b, c: (b * nc + c, 0)),
                  pl.BlockSpec((L, D_MLSTM), lambda b, c: (b * nc + c, 1)),
                  pl.BlockSpec((L, D_MLSTM), lambda b, c: (b * nc + c, 2)),
                  pl.BlockSpec((L, d), row),
                  pl.BlockSpec((None, d, N_GATE), lambda b, c: (layer, 0, 0)),
                  pl.BlockSpec((None, BF16_ROWS, d), lambda b, c: (layer, 0, 0)),
                  pl.BlockSpec((CONV_W, wq), const),
                  pl.BlockSpec((1, wq), const),
                  pl.BlockSpec((1, N_GATE), const),
                  pl.BlockSpec((2 * M_HEADS, 1), const),
                  pl.BlockSpec((1, D_MLSTM), const),
                  pl.BlockSpec((L, 3 * L), const),
                  pl.BlockSpec((3 * L, L), const)],
        out_specs=pl.BlockSpec((L, D_MLSTM), row),
        out_shape=jax.ShapeDtypeStruct((batch * seq, D_MLSTM), BF16),
        scratch_shapes=[pltpu.VMEM((CONV_HALO + L, wq), F32),
                        pltpu.VMEM((M_HEADS, M_DQK, M_DV), F32),
                        pltpu.VMEM((8, M_DQK), F32),
                        pltpu.VMEM((8, 128), F32)],
        compiler_params=_cparams(("arbitrary", "arbitrary")),
        name="mlstm",
    )(proj, proj, proj, u, wgc_all, wgr_all, conv_w, conv_b.reshape(1, wq), bias_c, bias_r,
      norm_w.reshape(1, D_MLSTM), tri3, trit3)


def _hgrn_chunk(q_ref, f_ref, i_ref, g_ref, lbl_ref, nw_ref, tri_ref, y_ref, st_ref, b_ref, *, L, layer):
    dh_all = H_HEADS * H_DH
    zero = lambda n: jnp.zeros((n, dh_all), F32)

    qf = q_ref[...]
    f = f_ref[...]
    qs = qf * _sigmoid(qf)
    e = jnp.exp(-jnp.abs(f))
    d = 1.0 + e
    log_f = jnp.minimum(f, 0.0) - jnp.log(d)
    kk = jnp.where(f >= 0.0, e, 1.0) / d
    if layer > 0:
        lg = lbl_ref[...]
        ex = jnp.exp(lg - jnp.max(lg, axis=0, keepdims=True))
        pr = ex / jnp.sum(ex, axis=0, keepdims=True)
        lb = pr[1:2, :]
        for j in range(2, layer + 1):
            lb = lb + pr[j:j + 1, :]
        c = jnp.log1p(-lb) + log_f
        a = jnp.log(lb)
        log_f = jnp.maximum(a, c) + jnp.log(1.0 + jnp.exp(-jnp.abs(a - c)))
        kk = (1.0 - lb) * kk

    lf2 = log_f * LOG2E
    b = _dot(tri_ref[...], jnp.concatenate(_split3(lf2), axis=0))
    b_ref[...] = b
    b_last = b[L - 1:L, :]
    qb = (qs * jnp.exp2(b)).astype(BF16)
    kd = (kk * jnp.exp2(b_last - b)).astype(BF16)
    dec = jnp.exp2(b_last)

    sub = lax.broadcasted_iota(jnp.int32, (SUBLANES, dh_all), 0)
    groups = range(L // SUBLANES)
    grp = lambda v, g: v[g * SUBLANES:(g + 1) * SUBLANES]
    odd = (sub & 1) != 0
    f2 = jnp.exp2(lf2)
    q_lv = [qs.astype(BF16),
            jnp.concatenate([jnp.where(odd, grp(qs, g) * grp(f2, g), 0.0) for g in groups], axis=0).astype(BF16)]
    k_lv = [kk.astype(BF16),
            jnp.concatenate([jnp.where(odd, 0.0, grp(kk, g)) for g in groups], axis=0).astype(BF16)]
    half = 2
    while half < L:
        qp, kp = [], []
        if half < SUBLANES:
            upper = (sub & half) != 0
            for g in groups:
                r = None
                for mblk in range(SUBLANES // (2 * half)):
                    cand = jnp.broadcast_to(b_ref[pl.ds(g * SUBLANES + mblk * 2 * half + half - 1, 1), :],
                                            (SUBLANES, dh_all))
                    r = cand if r is None else jnp.where(sub >= mblk * 2 * half, cand, r)
                fac = jnp.exp2(-jnp.abs(grp(b, g) - r))
                qp.append(jnp.where(upper, grp(qs, g) * fac, 0.0))
                kp.append(jnp.where(upper, 0.0, grp(kk, g) * fac))
        else:
            for j in range(L // (2 * half)):
                lo = slice(j * 2 * half, j * 2 * half + half)
                up = slice(j * 2 * half + half, (j + 1) * 2 * half)
                r = b_ref[pl.ds(j * 2 * half + half - 1, 1), :]
                kp += [kk[lo] * jnp.exp2(r - b[lo]), zero(half)]
                qp += [zero(half), qs[up] * jnp.exp2(b[up] - r)]
        q_lv.append(jnp.concatenate(qp, axis=0).astype(BF16))
        k_lv.append(jnp.concatenate(kp, axis=0).astype(BF16))
        half *= 2

    ti = lax.broadcasted_iota(jnp.int32, (L, L), 0)
    si = lax.broadcasted_iota(jnp.int32, (L, L), 1)
    txb = (ti ^ si).astype(F32).astype(BF16)

    atts = []
    for h in range(H_HEADS):
        sl = slice(h * H_DH, (h + 1) * H_DH)
        att = _dot_nt(q_lv[-1][:, sl], k_lv[-1][:, sl]).astype(BF16)
        for lev in range(len(q_lv) - 2, -1, -1):
            term = _dot_nt(q_lv[lev][:, sl], k_lv[lev][:, sl]).astype(BF16)
            att = jnp.where(txb < jnp.asarray(1 << lev, BF16), term, att)
        atts.append(att)

    for h in range(H_HEADS):
        sl = slice(h * H_DH, (h + 1) * H_DH)
        vb = i_ref[:, sl].astype(BF16)
        st = st_ref[h]
        o = _dot(atts[h], vb) + _dot_nt(qb[:, sl], st.astype(BF16))
        st_ref[h] = st * dec[:, sl] + _dot_tn(vb, kd[:, sl])
        on = o * lax.rsqrt(jnp.mean(o * o, axis=1, keepdims=True) + EPS)
        g = g_ref[:, sl]
        y_ref[:, sl] = (on * nw_ref[:, sl] * (g * _sigmoid(g))).astype(y_ref.dtype)


def _hgrn_kernel(q_ref, f_ref, i_ref, g_ref, lbl_ref, nw_ref, tri_ref, y_ref, st_ref, b_ref, *, L, layer):
    @pl.when(pl.program_id(1) == 0)
    def _():
        st_ref[...] = jnp.zeros(st_ref.shape, F32)

    for ck in range(q_ref.shape[0] // L):
        rows = pl.ds(ck * L, L)
        _hgrn_chunk(q_ref.at[rows], f_ref.at[rows], i_ref.at[rows], g_ref.at[rows], lbl_ref, nw_ref, tri_ref,
                    y_ref.at[rows], st_ref, b_ref.at[rows], L=L, layer=layer)


def _hgrn(proj, lb_logits, norm_w, layer, batch, seq):
    L = HGRN_CHUNK
    rows = HGRN_BLOCK
    nc = seq // rows
    tri3, _ = _tri_consts(L)
    const = lambda b, c: (0, 0)
    col = lambda j: pl.BlockSpec((rows, D_HGRN), lambda b, c: (b * nc + c, j))
    return pl.pallas_call(
        functools.partial(_hgrn_kernel, L=L, layer=layer),
        grid=(batch, nc),
        in_specs=[col(3), col(4), col(5), col(6),
                  pl.BlockSpec(lb_logits.shape, const),
                  pl.BlockSpec((1, D_HGRN), const),
                  pl.BlockSpec(tri3.shape, const)],
        out_specs=pl.BlockSpec((rows, D_HGRN), lambda b, c: (b * nc + c, 0)),
        out_shape=jax.ShapeDtypeStruct((batch * seq, D_HGRN), BF16),
        scratch_shapes=[pltpu.VMEM((H_HEADS, H_DH, H_DH), F32), pltpu.VMEM((rows, D_HGRN), F32)],
        compiler_params=_cparams(("arbitrary", "arbitrary")),
        name="hgrn2",
    )(proj, proj, proj, proj, lb_logits, norm_w.reshape(1, D_HGRN), tri3)


def _matmul_residual_ln(ychunk, x_ref, gate_ref, lng_ref, lnb_ref, sc_ref, sh_ref, xo_ref, uo_ref, z_ref,
                        *, sub, cw):
    tm, d = z_ref.shape
    nsub, nchunk = tm // sub, d // cw
    gp = (1.0 + gate_ref[0]) * (1.0 / ALPHA)
    lng, lnb = lng_ref[...], lnb_ref[...]
    if uo_ref is not None:
        ug = lng * (1.0 + sc_ref[0])
        ub = lnb * (1.0 + sc_ref[0]) + sh_ref[0]
    mu = None
    for s in range(nsub + 1):
        rows = pl.ds(s * sub, sub)
        prev = pl.ds((s - 1) * sub, sub)
        rsum = None
        for j in range(nchunk):
            cols = slice(j * cw, (j + 1) * cw)
            if s < nsub:
                z = x_ref[rows, cols] + gp[:, cols] * ychunk(rows, cols)
                z_ref[rows, cols] = z
                part = jnp.sum(z, axis=1, keepdims=True)
                rsum = part if rsum is None else rsum + part
            if s > 0:
                if j == 0:
                    zc = z_ref[prev, :] - mu
                    rstd = lax.rsqrt(jnp.mean(zc * zc, axis=1, keepdims=True) + EPS / (ALPHA * ALPHA))
                t = (z_ref[prev, cols] - mu) * rstd
                xo_ref[prev, cols] = t * lng[:, cols] + lnb[:, cols]
                if uo_ref is not None:
                    uo_ref[prev, cols] = (t * ug[:, cols] + ub[:, cols]).astype(uo_ref.dtype)
        if s < nsub:
            mu = rsum * (1.0 / d)


def _outproj_kernel(ym_ref, yh_ref, w_ref, x_ref, gate_ref, lng_ref, lnb_ref, sc_ref, sh_ref, xo_ref, uo_ref,
                    wb_ref, z_ref, *, sub, cw):
    @pl.when(pl.program_id(0) == 0)
    def _():
        wb_ref[...] = w_ref[...].astype(BF16)

    def ychunk(rows, cols):
        return (_dot(ym_ref[rows, :], wb_ref[0:D_MLSTM, cols]) +
                _dot(yh_ref[rows, :], wb_ref[D_MLSTM:D_MLSTM + D_HGRN, cols]))

    _matmul_residual_ln(ychunk, x_ref, gate_ref, lng_ref, lnb_ref, sc_ref, sh_ref, xo_ref, uo_ref, z_ref,
                        sub=sub, cw=cw)


def _outproj_ln(ym, yh, w_all, layer, x2, gate, ln_g, ln_b, sc, sh, seq):
    m, d = x2.shape
    wshape = w_all.shape[1:]
    tm, sub, cw = OUTPROJ_TILE
    per_b = seq // tm
    vec = pl.BlockSpec((1, 1, d), lambda i: (i // per_b, 0, 0))
    par = pl.BlockSpec((1, d), lambda i: (0, 0))
    rowblk = lambda width: pl.BlockSpec((tm, width), lambda i: (i, 0))
    return pl.pallas_call(
        functools.partial(_outproj_kernel, sub=sub, cw=cw),
        grid=(m // tm,),
        in_specs=[rowblk(D_MLSTM), rowblk(D_HGRN),
                  pl.BlockSpec((None,) + wshape, lambda i: (layer, 0, 0), pipeline_mode=pl.Buffered(1)),
                  rowblk(d), vec, par, par, vec, vec],
        out_specs=[rowblk(d), rowblk(d)],
        out_shape=[jax.ShapeDtypeStruct((m, d), F32), jax.ShapeDtypeStruct((m, d), BF16)],
        scratch_shapes=[pltpu.VMEM(wshape, BF16), pltpu.VMEM((tm, d), F32)],
        compiler_params=_cparams(("arbitrary",)),
        name="outproj_ln",
    )(ym, yh, w_all, x2, gate, ln_g.reshape(1, d), ln_b.reshape(1, d), sc, sh)


def _ffn_up_kernel(u_ref, wg_ref, wu_ref, h_ref, wgb_ref, wub_ref, *, sub):
    @pl.when(pl.program_id(1) == 0)
    def _():
        wgb_ref[...] = wg_ref[...].astype(BF16)
        wub_ref[...] = wu_ref[...].astype(BF16)

    for r in range(h_ref.shape[0] // sub):
        rows = pl.ds(r * sub, sub)
        u = u_ref[rows, :]
        g = _dot(u, wgb_ref[...])
        h_ref[rows, :] = (g * _sigmoid(g) * _dot(u, wub_ref[...])).astype(h_ref.dtype)


def _ffn_up(u, wg_all, wu_all, layer):
    m, k = u.shape
    n = wg_all.shape[2]
    tm, tn, sub = FFN_UP_TILE
    wspec = pl.BlockSpec((None, k, tn), lambda j, i: (layer, 0, j))
    return pl.pallas_call(
        functools.partial(_ffn_up_kernel, sub=sub),
        grid=(n // tn, m // tm),
        in_specs=[pl.BlockSpec((tm, k), lambda j, i: (i, 0)), wspec, wspec],
        out_specs=pl.BlockSpec((tm, tn), lambda j, i: (i, j)),
        out_shape=jax.ShapeDtypeStruct((m, n), BF16),
        scratch_shapes=[pltpu.VMEM((k, tn), BF16), pltpu.VMEM((k, tn), BF16)],
        compiler_params=_cparams(("arbitrary", "arbitrary")),
        name="ffn_up",
    )(u, wg_all, wu_all)


def _down_kernel(h_ref, w_ref, x_ref, gate_ref, lng_ref, lnb_ref, sc_ref, sh_ref, *refs, sub, cw, emit_u):
    xo_ref = refs[0]
    uo_ref = refs[1] if emit_u else None
    ychunk = lambda rows, cols: _dot(h_ref[rows, :], w_ref[:, cols])
    _matmul_residual_ln(ychunk, x_ref, gate_ref, lng_ref, lnb_ref, sc_ref, sh_ref, xo_ref, uo_ref, xo_ref,
                        sub=sub, cw=cw)


def _down_ln(hid, w_all, layer, x2, gate, ln_g, ln_b, sc, sh, seq, emit_u):
    m, d = x2.shape
    kdim = hid.shape[1]
    tm, sub, cw = FFN_DOWN_TILE
    per_b = seq // tm
    vec = pl.BlockSpec((1, 1, d), lambda i: (i // per_b, 0, 0))
    par = pl.BlockSpec((1, d), lambda i: (0, 0))
    rowblk = pl.BlockSpec((tm, d), lambda i: (i, 0))
    n_out = 2 if emit_u else 1
    return pl.pallas_call(
        functools.partial(_down_kernel, sub=sub, cw=cw, emit_u=emit_u),
        grid=(m // tm,),
        in_specs=[pl.BlockSpec((tm, kdim), lambda i: (i, 0)),
                  pl.BlockSpec((None,) + w_all.shape[1:], lambda i: (layer, 0, 0), pipeline_mode=pl.Buffered(1)),
                  rowblk, vec, par, par, vec, vec],
        out_specs=[rowblk, rowblk][:n_out],
        out_shape=[jax.ShapeDtypeStruct((m, d), F32), jax.ShapeDtypeStruct((m, d), BF16)][:n_out],
        compiler_params=_cparams(("arbitrary",)),
        name="ffn_down_ln",
    )(hid, w_all, x2, gate, ln_g.reshape(1, d), ln_b.reshape(1, d), sc, sh)


def _prep_w_in(w_in):
    g0 = 3 * 1024
    g1 = g0 + 2 * M_HEADS
    big = jnp.concatenate([w_in[:, :, :g0], w_in[:, :, g1:]], axis=2).astype(BF16)
    gates = w_in[:, :, g0:g1]
    wgc = jnp.pad(gates, ((0, 0), (0, 0), (0, N_GATE - 2 * M_HEADS))).astype(BF16)
    wgr = jnp.pad(jnp.swapaxes(gates, 1, 2), ((0, 0), (0, BF16_ROWS - 2 * M_HEADS), (0, 0))).astype(BF16)
    return big, wgc, wgr


def kernel(x, c, w_mod, b_mod, w_in, conv_w, conv_b, b_igate, b_fgate, mlstm_norm_w, hgrn_norm_w, lb_logits, w_out, ln1_g, ln1_b, w_gate, w_up, w_down, ln2_g, ln2_b):
    batch, seq, d = x.shape
    depth = w_mod.shape[0]
    m = batch * seq
    x2 = x.reshape(m, d)

    c_pad = jnp.pad(c, ((0, 8 - batch), (0, 0)))
    mod = _mod_all(c_pad, w_mod, b_mod)[:, :batch]
    mod = mod.reshape(depth, batch, N_MOD, 1, d)
    mvec = lambda l, k: mod[l, :, k]

    w_big, wgc, wgr = _prep_w_in(w_in)
    w_down_bf = w_down.astype(BF16)
    u = _modulate(x2, mvec(0, 1), mvec(0, 0), seq)
    for l in range(depth):
        proj = _matmul(u, w_big, l, F32, *IN_PROJ_TILE, "in_proj")
        ym = _mlstm(proj, u, wgc, wgr, l, conv_w[l], conv_b[l], b_igate[l], b_fgate[l], mlstm_norm_w[l],
                    batch, seq)
        yh = _hgrn(proj, lb_logits, hgrn_norm_w[l], l, batch, seq)
        x2, u = _outproj_ln(ym, yh, w_out, l, x2, mvec(l, 2), ln1_g[l], ln1_b[l], mvec(l, 4), mvec(l, 3), seq)
        hid = _ffn_up(u, w_gate, w_up, l)
        last = l == depth - 1
        nl = l if last else l + 1
        outs = _down_ln(hid, w_down_bf, l, x2, mvec(l, 5), ln2_g[l], ln2_b[l],
                        mvec(nl, 1), mvec(nl, 0), seq, not last)
        x2 = outs[0]
        u = None if last else outs[1]
    return x2.reshape(batch, seq, d)
```

```python
import functools

import numpy as np
import jax
import jax.numpy as jnp
from jax import lax
from jax.experimental import pallas as pl
from jax.experimental.pallas import tpu as pltpu

F32 = jnp.float32
BF16 = jnp.bfloat16

DEPTH = 4
D_MLSTM = 1024
D_HGRN = 1024
M_HEADS = 4
M_DV = 256
M_DQK = 128
H_HEADS = 8
H_DH = 128
CONV_W = 4
N_MOD = 6
EPS = 1e-5
ALPHA = (2 * DEPTH) ** 0.25
N_GATE = 128
SUBLANES = 8
BF16_ROWS = 16
CONV_HALO = SUBLANES
LOG2E = 1.4426950408889634

VMEM_LIMIT = 56 * 1024 * 1024

MLSTM_CHUNK = 256
HGRN_CHUNK = 128
HGRN_BLOCK = 512

MOD_COLS = 2048
MODULATE_ROWS = 1024
IN_PROJ_TILE = (2048, 1024)
OUTPROJ_TILE = (512, 128, 512)
FFN_UP_TILE = (2048, 512, 256)
FFN_DOWN_TILE = (256, 128, 512)


def _cparams(sem):
    return pltpu.CompilerParams(dimension_semantics=sem, vmem_limit_bytes=VMEM_LIMIT)


def _sigmoid(x):
    return 1.0 / (1.0 + jnp.exp(-x))


def _log_sigmoid(x):
    return jnp.minimum(x, 0.0) - jnp.log(1.0 + jnp.exp(-jnp.abs(x)))


def _split3(x):
    hi = x.astype(BF16)
    r1 = x - hi.astype(F32)
    mid = r1.astype(BF16)
    lo = (r1 - mid.astype(F32)).astype(BF16)
    return hi, mid, lo


def _dot(a, b):
    return jnp.dot(a, b, preferred_element_type=F32)


def _dot_nt(a, b):
    return lax.dot_general(a, b, (((1,), (1,)), ((), ())), preferred_element_type=F32)


def _dot_tn(a, b):
    return lax.dot_general(a, b, (((0,), (0,)), ((), ())), preferred_element_type=F32)


def _mod_kernel(c_ref, w_ref, b_ref, o_ref):
    c = c_ref[...]
    ca = (c * _sigmoid(c)).astype(BF16)
    o_ref[0] = _dot(ca, w_ref[0].astype(BF16)) + b_ref[0]


def _mod_all(c_pad, w_mod, b_mod):
    depth, d, n = w_mod.shape
    tn = MOD_COLS
    rows = c_pad.shape[0]
    return pl.pallas_call(
        _mod_kernel,
        grid=(depth, n // tn),
        in_specs=[pl.BlockSpec((rows, d), lambda l, j: (0, 0)),
                  pl.BlockSpec((1, d, tn), lambda l, j: (l, 0, j)),
                  pl.BlockSpec((1, 1, tn), lambda l, j: (l, 0, j))],
        out_specs=pl.BlockSpec((1, rows, tn), lambda l, j: (l, 0, j)),
        out_shape=jax.ShapeDtypeStruct((depth, rows, n), F32),
        compiler_params=_cparams(("arbitrary", "arbitrary")),
        name="adaln_mod",
    )(c_pad, w_mod, b_mod.reshape(depth, 1, n))


def _modulate_kernel(x_ref, sc_ref, sh_ref, u_ref):
    u_ref[...] = (x_ref[...] * (1.0 + sc_ref[0]) + sh_ref[0]).astype(u_ref.dtype)


def _modulate(x2, sc, sh, seq):
    m, d = x2.shape
    tm = MODULATE_ROWS
    per_b = seq // tm
    vec = pl.BlockSpec((1, 1, d), lambda i: (i // per_b, 0, 0))
    return pl.pallas_call(
        _modulate_kernel,
        grid=(m // tm,),
        in_specs=[pl.BlockSpec((tm, d), lambda i: (i, 0)), vec, vec],
        out_specs=pl.BlockSpec((tm, d), lambda i: (i, 0)),
        out_shape=jax.ShapeDtypeStruct((m, d), BF16),
        compiler_params=_cparams(("arbitrary",)),
        name="modulate0",
    )(x2, sc, sh)


def _mm_kernel(x_ref, w_ref, o_ref):
    o_ref[...] = _dot(x_ref[...], w_ref[...]).astype(o_ref.dtype)


def _matmul(x, w_all, layer, out_dtype, tm, tn, name):
    m, k = x.shape
    n = w_all.shape[2]
    return pl.pallas_call(
        _mm_kernel,
        grid=(n // tn, m // tm),
        in_specs=[pl.BlockSpec((tm, k), lambda j, i: (i, 0)),
                  pl.BlockSpec((None, k, tn), lambda j, i: (layer, 0, j))],
        out_specs=pl.BlockSpec((tm, tn), lambda j, i: (i, j)),
        out_shape=jax.ShapeDtypeStruct((m, n), out_dtype),
        compiler_params=_cparams(("arbitrary", "arbitrary")),
        name=name,
    )(x, w_all)


def _mlstm_kernel(qk_ref, v_ref, og_ref, u_ref, wgc_ref, wgr_ref, cw_ref, cb_ref, bc_ref, br_ref, nw_ref,
                  tri_ref, trit_ref, y_ref, xp_ref, c_ref, n_ref, m_ref, *, L):
    dqk_all = M_HEADS * M_DQK

    @pl.when(pl.program_id(1) == 0)
    def _():
        xp_ref[0:CONV_HALO, :] = jnp.zeros((CONV_HALO, 2 * dqk_all), F32)
        c_ref[...] = jnp.zeros(c_ref.shape, F32)
        n_ref[...] = jnp.zeros(n_ref.shape, F32)
        m_ref[...] = jnp.zeros(m_ref.shape, F32)

    x = qk_ref[...]
    xp_ref[CONV_HALO:CONV_HALO + L, :] = x
    y = cb_ref[...]
    for j in range(CONV_W):
        off = CONV_HALO - (CONV_W - 1) + j
        y = y + cw_ref[j:j + 1, :] * xp_ref[off:off + L, :]
    xp_ref[0:CONV_HALO, :] = x[L - CONV_HALO:L, :]
    qk = y * _sigmoid(y)

    u = u_ref[...]
    gc = _dot(u, wgc_ref[...]) + bc_ref[...]
    gr = _dot_nt(wgr_ref[...], u)[0:2 * M_HEADS, :] + br_ref[...]
    bcum_c = _dot(tri_ref[...], jnp.concatenate(_split3(_log_sigmoid(gc) * LOG2E), axis=0))
    bcum_r = _dot(jnp.concatenate(_split3(_log_sigmoid(gr) * LOG2E), axis=1), trit_ref[...])
    gc = gc * LOG2E
    gr = gr * LOG2E

    ti = lax.broadcasted_iota(jnp.int32, (L, L), 0)
    si = lax.broadcasted_iota(jnp.int32, (L, L), 1)
    causal = si <= ti

    for h in range(M_HEADS):
        q = qk[:, h * M_DQK:(h + 1) * M_DQK] * (M_DQK ** -0.5)
        k = qk[:, dqk_all + h * M_DQK: dqk_all + (h + 1) * M_DQK]
        qb = q.astype(BF16)
        kb = k.astype(BF16)
        vb = v_ref[:, h * M_DV:(h + 1) * M_DV].astype(BF16)
        bc = bcum_c[:, M_HEADS + h:M_HEADS + h + 1]
        ic = gc[:, h:h + 1]
        br = bcum_r[M_HEADS + h:M_HEADS + h + 1, :]
        ir = gr[h:h + 1, :]
        c_prev = c_ref[h]
        n_prev = n_ref[h:h + 1, :]
        m_prev = m_ref[h:h + 1, 0:1]

        log_d = jnp.where(causal, bc - br + ir, -jnp.inf)
        m_inter = bc + m_prev
        m_t = jnp.maximum(m_inter, jnp.max(log_d, axis=1, keepdims=True))
        dw = jnp.exp2(log_d - m_t)
        w_inter = jnp.exp2(m_inter - m_t)
        s = _dot_nt(qb, kb) * dw
        num = _dot(s.astype(BF16), vb) + w_inter * _dot(qb, c_prev.astype(BF16))
        den = jnp.sum(s, axis=1, keepdims=True) + w_inter * jnp.sum(q * n_prev, axis=1, keepdims=True)
        hh = num * (1.0 / jnp.maximum(jnp.abs(den), jnp.exp2(-m_t)))

        b_last = bc[L - 1:L, :]
        lwe = b_last - bc + ic
        m_new = jnp.maximum(b_last + m_prev, jnp.max(lwe, axis=0, keepdims=True))
        w_end = jnp.exp2(lwe - m_new)
        decay = jnp.exp2(b_last + m_prev - m_new)
        kw = k * w_end
        c_ref[h] = decay * c_prev + _dot_tn(kw.astype(BF16), vb)
        n_ref[h:h + 1, :] = decay * n_prev + jnp.sum(kw, axis=0, keepdims=True)
        m_ref[h:h + 1, :] = jnp.broadcast_to(m_new, (1, m_ref.shape[1]))

        hc = hh - jnp.mean(hh, axis=1, keepdims=True)
        hn = hc * lax.rsqrt(jnp.mean(hc * hc, axis=1, keepdims=True) + EPS)
        sl = slice(h * M_DV, (h + 1) * M_DV)
        y_ref[:, sl] = (hn * nw_ref[:, sl] * _sigmoid(og_ref[:, sl])).astype(y_ref.dtype)


def _tri_consts(L):
    tri = np.tril(np.ones((L, L), np.float32))
    tri3 = np.concatenate([tri, tri, tri], axis=1)
    trit3 = np.concatenate([tri.T, tri.T, tri.T], axis=0)
    return jnp.asarray(tri3, BF16), jnp.asarray(trit3, BF16)


def _mlstm(proj, u, wgc_all, wgr_all, layer, conv_w, conv_b, b_ig, b_fg, norm_w, batch, seq):
    L = MLSTM_CHUNK
    nc = seq // L
    wq = 2 * M_HEADS * M_DQK
    tri3, trit3 = _tri_consts(L)
    bias_c = jnp.zeros((1, N_GATE), F32).at[0, :M_HEADS].set(b_ig).at[0, M_HEADS:2 * M_HEADS].set(b_fg)
    bias_r = jnp.concatenate([b_ig, b_fg]).reshape(2 * M_HEADS, 1)
    d = u.shape[1]
    row = lambda b, c: (b * nc + c, 0)
    const = lambda b, c: (0, 0)
    return pl.pallas_call(
        functools.partial(_mlstm_kernel, L=L),
        grid=(batch, nc),
        in_specs=[pl.BlockSpec((L, wq), lambda b, c: (b * nc + c, 0)),
                  pl.BlockSpec((L, D_MLSTM), lambda b, c: (b * nc + c, 1)),
                  pl.BlockSpec((L, D_MLSTM), lambda b, c: (b * nc + c, 2)),
                  pl.BlockSpec((L, d), row),
                  pl.BlockSpec((None, d, N_GATE), lambda b, c: (layer, 0, 0)),
                  pl.BlockSpec((None, BF16_ROWS, d), lambda b, c: (layer, 0, 0)),
                  pl.BlockSpec((CONV_W, wq), const),
                  pl.BlockSpec((1, wq), const),
                  pl.BlockSpec((1, N_GATE), const),
                  pl.BlockSpec((2 * M_HEADS, 1), const),
                  pl.BlockSpec((1, D_MLSTM), const),
                  pl.BlockSpec((L, 3 * L), const),
                  pl.BlockSpec((3 * L, L), const)],
        out_specs=pl.BlockSpec((L, D_MLSTM), row),
        out_shape=jax.ShapeDtypeStruct((batch * seq, D_MLSTM), BF16),
        scratch_shapes=[pltpu.VMEM((CONV_HALO + L, wq), F32),
                        pltpu.VMEM((M_HEADS, M_DQK, M_DV), F32),
                        pltpu.VMEM((8, M_DQK), F32),
                        pltpu.VMEM((8, 128), F32)],
        compiler_params=_cparams(("arbitrary", "arbitrary")),
        name="mlstm",
    )(proj, proj, proj, u, wgc_all, wgr_all, conv_w, conv_b.reshape(1, wq), bias_c, bias_r,
      norm_w.reshape(1, D_MLSTM), tri3, trit3)


def _hgrn_chunk(q_ref, f_ref, i_ref, g_ref, lbl_ref, nw_ref, tri_ref, y_ref, st_ref, b_ref, *, L, layer):
    dh_all = H_HEADS * H_DH
    zero = lambda n: jnp.zeros((n, dh_all), F32)

    qf = q_ref[...]
    f = f_ref[...]
    qs = qf * _sigmoid(qf)
    e = jnp.exp(-jnp.abs(f))
    d = 1.0 + e
    log_f = jnp.minimum(f, 0.0) - jnp.log(d)
    kk = jnp.where(f >= 0.0, e, 1.0) / d
    if layer > 0:
        lg = lbl_ref[...]
        ex = jnp.exp(lg - jnp.max(lg, axis=0, keepdims=True))
        pr = ex / jnp.sum(ex, axis=0, keepdims=True)
        lb = pr[1:2, :]
        for j in range(2, layer + 1):
            lb = lb + pr[j:j + 1, :]
        c = jnp.log1p(-lb) + log_f
        a = jnp.log(lb)
        log_f = jnp.maximum(a, c) + jnp.log(1.0 + jnp.exp(-jnp.abs(a - c)))
        kk = (1.0 - lb) * kk

    lf2 = log_f * LOG2E
    b = _dot(tri_ref[...], jnp.concatenate(_split3(lf2), axis=0))
    b_ref[...] = b
    b_last = b[L - 1:L, :]
    qb = (qs * jnp.exp2(b)).astype(BF16)
    kd = (kk * jnp.exp2(b_last - b)).astype(BF16)
    dec = jnp.exp2(b_last)

    sub = lax.broadcasted_iota(jnp.int32, (SUBLANES, dh_all), 0)
    groups = range(L // SUBLANES)
    grp = lambda v, g: v[g * SUBLANES:(g + 1) * SUBLANES]
    odd = (sub & 1) != 0
    f2 = jnp.exp2(lf2)
    q_lv = [qs.astype(BF16),
            jnp.concatenate([jnp.where(odd, grp(qs, g) * grp(f2, g), 0.0) for g in groups], axis=0).astype(BF16)]
    k_lv = [kk.astype(BF16),
            jnp.concatenate([jnp.where(odd, 0.0, grp(kk, g)) for g in groups], axis=0).astype(BF16)]
    half = 2
    while half < L:
        qp, kp = [], []
        if half < SUBLANES:
            upper = (sub & half) != 0
            for g in groups:
                r = None
                for mblk in range(SUBLANES // (2 * half)):
                    cand = jnp.broadcast_to(b_ref[pl.ds(g * SUBLANES + mblk * 2 * half + half - 1, 1), :],
                                            (SUBLANES, dh_all))
                    r = cand if r is None else jnp.where(sub >= mblk * 2 * half, cand, r)
                fac = jnp.exp2(-jnp.abs(grp(b, g) - r))
                qp.append(jnp.where(upper, grp(qs, g) * fac, 0.0))
                kp.append(jnp.where(upper, 0.0, grp(kk, g) * fac))
        else:
            for j in range(L // (2 * half)):
                lo = slice(j * 2 * half, j * 2 * half + half)
                up = slice(j * 2 * half + half, (j + 1) * 2 * half)
                r = b_ref[pl.ds(j * 2 * half + half - 1, 1), :]
                kp += [kk[lo] * jnp.exp2(r - b[lo]), zero(half)]
                qp += [zero(half), qs[up] * jnp.exp2(b[up] - r)]
        q_lv.append(jnp.concatenate(qp, axis=0).astype(BF16))
        k_lv.append(jnp.concatenate(kp, axis=0).astype(BF16))
        half *= 2

    ti = lax.broadcasted_iota(jnp.int32, (L, L), 0)
    si = lax.broadcasted_iota(jnp.int32, (L, L), 1)
    txb = (ti ^ si).astype(F32).astype(BF16)

    atts = []
    for h in range(H_HEADS):
        sl = slice(h * H_DH, (h + 1) * H_DH)
        att = _dot_nt(q_lv[-1][:, sl], k_lv[-1][:, sl]).astype(BF16)
        for lev in range(len(q_lv) - 2, -1, -1):
            term = _dot_nt(q_lv[lev][:, sl], k_lv[lev][:, sl]).astype(BF16)
            att = jnp.where(txb < jnp.asarray(1 << lev, BF16), term, att)
        atts.append(att)

    for h in range(H_HEADS):
        sl = slice(h * H_DH, (h + 1) * H_DH)
        vb = i_ref[:, sl].astype(BF16)
        st = st_ref[h]
        o = _dot(atts[h], vb) + _dot_nt(qb[:, sl], st.astype(BF16))
        st_ref[h] = st * dec[:, sl] + _dot_tn(vb, kd[:, sl])
        on = o * lax.rsqrt(jnp.mean(o * o, axis=1, keepdims=True) + EPS)
        g = g_ref[:, sl]
        y_ref[:, sl] = (on * nw_ref[:, sl] * (g * _sigmoid(g))).astype(y_ref.dtype)


HGRN_STREAMS = 4
HGRN_RING = 3


def _hgrn_kernel(proj_ref, lbl_ref, nw_ref, tri_ref, y_ref, st_ref, b_ref, buf_ref, sem, *, L, layer, col0):
    rows = buf_ref.shape[2]
    nc = pl.num_programs(1)
    t = pl.program_id(0) * nc + pl.program_id(1)
    n = pl.num_programs(0) * nc

    def copies(step):
        slot = lax.rem(step, HGRN_RING)
        r0 = pl.multiple_of(step * rows, rows)
        return [pltpu.make_async_copy(proj_ref.at[pl.ds(r0, rows), pl.ds((col0 + s) * D_HGRN, D_HGRN)],
                                      buf_ref.at[s, slot], sem.at[s, slot]) for s in range(HGRN_STREAMS)]

    @pl.when(t == 0)
    def _():
        for ahead in range(HGRN_RING - 1):
            for cp in copies(ahead):
                cp.start()

    @pl.when(t + (HGRN_RING - 1) < n)
    def _():
        for cp in copies(t + (HGRN_RING - 1)):
            cp.start()

    for cp in copies(t):
        cp.wait()

    @pl.when(pl.program_id(1) == 0)
    def _():
        st_ref[...] = jnp.zeros(st_ref.shape, F32)

    slot = lax.rem(t, HGRN_RING)
    q_ref, f_ref, i_ref, g_ref = (buf_ref.at[s, slot] for s in range(HGRN_STREAMS))
    for ck in range(rows // L):
        rs = pl.ds(ck * L, L)
        _hgrn_chunk(q_ref.at[rs], f_ref.at[rs], i_ref.at[rs], g_ref.at[rs], lbl_ref, nw_ref, tri_ref,
                    y_ref.at[rs], st_ref, b_ref.at[rs], L=L, layer=layer)


def _hgrn(proj, lb_logits, norm_w, layer, batch, seq):
    L = HGRN_CHUNK
    rows = HGRN_BLOCK
    nc = seq // rows
    assert batch * nc >= HGRN_RING - 1
    tri3, _ = _tri_consts(L)
    const = lambda b, c: (0, 0)
    return pl.pallas_call(
        functools.partial(_hgrn_kernel, L=L, layer=layer, col0=3),
        grid=(batch, nc),
        in_specs=[pl.BlockSpec(memory_space=pl.ANY),
                  pl.BlockSpec(lb_logits.shape, const),
                  pl.BlockSpec((1, D_HGRN), const),
                  pl.BlockSpec(tri3.shape, const)],
        out_specs=pl.BlockSpec((rows, D_HGRN), lambda b, c: (b * nc + c, 0)),
        out_shape=jax.ShapeDtypeStruct((batch * seq, D_HGRN), BF16),
        scratch_shapes=[pltpu.VMEM((H_HEADS, H_DH, H_DH), F32), pltpu.VMEM((rows, D_HGRN), F32),
                        pltpu.VMEM((HGRN_STREAMS, HGRN_RING, rows, D_HGRN), F32),
                        pltpu.SemaphoreType.DMA((HGRN_STREAMS, HGRN_RING))],
        compiler_params=_cparams(("arbitrary", "arbitrary")),
        name="hgrn2",
    )(proj, lb_logits, norm_w.reshape(1, D_HGRN), tri3)


def _matmul_residual_ln(ychunk, x_ref, gate_ref, lng_ref, lnb_ref, sc_ref, sh_ref, xo_ref, uo_ref, z_ref,
                        *, sub, cw):
    tm, d = z_ref.shape
    nsub, nchunk = tm // sub, d // cw
    gp = (1.0 + gate_ref[0]) * (1.0 / ALPHA)
    lng, lnb = lng_ref[...], lnb_ref[...]
    if uo_ref is not None:
        ug = lng * (1.0 + sc_ref[0])
        ub = lnb * (1.0 + sc_ref[0]) + sh_ref[0]
    mu = None
    for s in range(nsub + 1):
        rows = pl.ds(s * sub, sub)
        prev = pl.ds((s - 1) * sub, sub)
        rsum = None
        for j in range(nchunk):
            cols = slice(j * cw, (j + 1) * cw)
            if s < nsub:
                z = x_ref[rows, cols] + gp[:, cols] * ychunk(rows, cols)
                z_ref[rows, cols] = z
                part = jnp.sum(z, axis=1, keepdims=True)
                rsum = part if rsum is None else rsum + part
            if s > 0:
                if j == 0:
                    zc = z_ref[prev, :] - mu
                    rstd = lax.rsqrt(jnp.mean(zc * zc, axis=1, keepdims=True) + EPS / (ALPHA * ALPHA))
                t = (z_ref[prev, cols] - mu) * rstd
                xo_ref[prev, cols] = t * lng[:, cols] + lnb[:, cols]
                if uo_ref is not None:
                    uo_ref[prev, cols] = (t * ug[:, cols] + ub[:, cols]).astype(uo_ref.dtype)
        if s < nsub:
            mu = rsum * (1.0 / d)


def _outproj_kernel(ym_ref, yh_ref, w_ref, x_ref, gate_ref, lng_ref, lnb_ref, sc_ref, sh_ref, xo_ref, uo_ref,
                    wb_ref, z_ref, *, sub, cw):
    @pl.when(pl.program_id(0) == 0)
    def _():
        wb_ref[...] = w_ref[...].astype(BF16)

    def ychunk(rows, cols):
        return (_dot(ym_ref[rows, :], wb_ref[0:D_MLSTM, cols]) +
                _dot(yh_ref[rows, :], wb_ref[D_MLSTM:D_MLSTM + D_HGRN, cols]))

    _matmul_residual_ln(ychunk, x_ref, gate_ref, lng_ref, lnb_ref, sc_ref, sh_ref, xo_ref, uo_ref, z_ref,
                        sub=sub, cw=cw)


def _outproj_ln(ym, yh, w_all, layer, x2, gate, ln_g, ln_b, sc, sh, seq):
    m, d = x2.shape
    wshape = w_all.shape[1:]
    tm, sub, cw = OUTPROJ_TILE
    per_b = seq // tm
    vec = pl.BlockSpec((1, 1, d), lambda i: (i // per_b, 0, 0))
    par = pl.BlockSpec((1, d), lambda i: (0, 0))
    rowblk = lambda width: pl.BlockSpec((tm, width), lambda i: (i, 0))
    return pl.pallas_call(
        functools.partial(_outproj_kernel, sub=sub, cw=cw),
        grid=(m // tm,),
        in_specs=[rowblk(D_MLSTM), rowblk(D_HGRN),
                  pl.BlockSpec((None,) + wshape, lambda i: (layer, 0, 0), pipeline_mode=pl.Buffered(1)),
                  rowblk(d), vec, par, par, vec, vec],
        out_specs=[rowblk(d), rowblk(d)],
        out_shape=[jax.ShapeDtypeStruct((m, d), F32), jax.ShapeDtypeStruct((m, d), BF16)],
        scratch_shapes=[pltpu.VMEM(wshape, BF16), pltpu.VMEM((tm, d), F32)],
        compiler_params=_cparams(("arbitrary",)),
        name="outproj_ln",
    )(ym, yh, w_all, x2, gate, ln_g.reshape(1, d), ln_b.reshape(1, d), sc, sh)


def _ffn_up_kernel(u_ref, wg_ref, wu_ref, h_ref, wgb_ref, wub_ref, *, sub):
    @pl.when(pl.program_id(1) == 0)
    def _():
        wgb_ref[...] = wg_ref[...].astype(BF16)
        wub_ref[...] = wu_ref[...].astype(BF16)

    for r in range(h_ref.shape[0] // sub):
        rows = pl.ds(r * sub, sub)
        u = u_ref[rows, :]
        g = _dot(u, wgb_ref[...])
        h_ref[rows, :] = (g * _sigmoid(g) * _dot(u, wub_ref[...])).astype(h_ref.dtype)


def _ffn_up(u, wg_all, wu_all, layer):
    m, k = u.shape
    n = wg_all.shape[2]
    tm, tn, sub = FFN_UP_TILE
    wspec = pl.BlockSpec((None, k, tn), lambda j, i: (layer, 0, j))
    return pl.pallas_call(
        functools.partial(_ffn_up_kernel, sub=sub),
        grid=(n // tn, m // tm),
        in_specs=[pl.BlockSpec((tm, k), lambda j, i: (i, 0)), wspec, wspec],
        out_specs=pl.BlockSpec((tm, tn), lambda j, i: (i, j)),
        out_shape=jax.ShapeDtypeStruct((m, n), BF16),
        scratch_shapes=[pltpu.VMEM((k, tn), BF16), pltpu.VMEM((k, tn), BF16)],
        compiler_params=_cparams(("arbitrary", "arbitrary")),
        name="ffn_up",
    )(u, wg_all, wu_all)


def _down_kernel(h_ref, w_ref, x_ref, gate_ref, lng_ref, lnb_ref, sc_ref, sh_ref, *refs, sub, cw, emit_u):
    xo_ref = refs[0]
    uo_ref = refs[1] if emit_u else None
    z_ref = refs[-1]
    ychunk = lambda rows, cols: _dot(h_ref[rows, :], w_ref[:, cols])
    _matmul_residual_ln(ychunk, x_ref, gate_ref, lng_ref, lnb_ref, sc_ref, sh_ref, xo_ref, uo_ref, z_ref,
                        sub=sub, cw=cw)


def _down_ln(hid, w_all, layer, x2, gate, ln_g, ln_b, sc, sh, seq, emit_u):
    m, d = x2.shape
    kdim = hid.shape[1]
    tm, sub, cw = FFN_DOWN_TILE
    per_b = seq // tm
    vec = pl.BlockSpec((1, 1, d), lambda i: (i // per_b, 0, 0))
    par = pl.BlockSpec((1, d), lambda i: (0, 0))
    rowblk = pl.BlockSpec((tm, d), lambda i: (i, 0))
    n_out = 2 if emit_u else 1
    return pl.pallas_call(
        functools.partial(_down_kernel, sub=sub, cw=cw, emit_u=emit_u),
        grid=(m // tm,),
        in_specs=[pl.BlockSpec((tm, kdim), lambda i: (i, 0)),
                  pl.BlockSpec((None,) + w_all.shape[1:], lambda i: (layer, 0, 0), pipeline_mode=pl.Buffered(1)),
                  rowblk, vec, par, par, vec, vec],
        out_specs=[rowblk, rowblk][:n_out],
        out_shape=[jax.ShapeDtypeStruct((m, d), F32), jax.ShapeDtypeStruct((m, d), BF16)][:n_out],
        scratch_shapes=[pltpu.VMEM((tm, d), F32)],
        compiler_params=_cparams(("arbitrary",)),
        name="ffn_down_ln",
    )(hid, w_all, x2, gate, ln_g.reshape(1, d), ln_b.reshape(1, d), sc, sh)


def _prep_w_in(w_in):
    g0 = 3 * 1024
    g1 = g0 + 2 * M_HEADS
    big = jnp.concatenate([w_in[:, :, :g0], w_in[:, :, g1:]], axis=2).astype(BF16)
    gates = w_in[:, :, g0:g1]
    wgc = jnp.pad(gates, ((0, 0), (0, 0), (0, N_GATE - 2 * M_HEADS))).astype(BF16)
    wgr = jnp.pad(jnp.swapaxes(gates, 1, 2), ((0, 0), (0, BF16_ROWS - 2 * M_HEADS), (0, 0))).astype(BF16)
    return big, wgc, wgr


def kernel(x, c, w_mod, b_mod, w_in, conv_w, conv_b, b_igate, b_fgate, mlstm_norm_w, hgrn_norm_w, lb_logits, w_out, ln1_g, ln1_b, w_gate, w_up, w_down, ln2_g, ln2_b):
    batch, seq, d = x.shape
    depth = w_mod.shape[0]
    m = batch * seq
    x2 = x.reshape(m, d)

    c_pad = jnp.pad(c, ((0, 8 - batch), (0, 0)))
    mod = _mod_all(c_pad, w_mod, b_mod)[:, :batch]
    mod = mod.reshape(depth, batch, N_MOD, 1, d)
    mvec = lambda l, k: mod[l, :, k]

    w_big, wgc, wgr = _prep_w_in(w_in)
    w_down_bf = w_down.astype(BF16)
    u = _modulate(x2, mvec(0, 1), mvec(0, 0), seq)
    for l in range(depth):
        proj = _matmul(u, w_big, l, F32, *IN_PROJ_TILE, "in_proj")
        ym = _mlstm(proj, u, wgc, wgr, l, conv_w[l], conv_b[l], b_igate[l], b_fgate[l], mlstm_norm_w[l],
                    batch, seq)
        yh = _hgrn(proj, lb_logits, hgrn_norm_w[l], l, batch, seq)
        x2, u = _outproj_ln(ym, yh, w_out, l, x2, mvec(l, 2), ln1_g[l], ln1_b[l], mvec(l, 4), mvec(l, 3), seq)
        hid = _ffn_up(u, w_gate, w_up, l)
        last = l == depth - 1
        nl = l if last else l + 1
        outs = _down_ln(hid, w_down_bf, l, x2, mvec(l, 5), ln2_g[l], ln2_b[l],
                        mvec(nl, 1), mvec(nl, 0), seq, not last)
        x2 = outs[0]
        u = None if last else outs[1]
    return x2.reshape(batch, seq, d)
```

```python
import functools

import numpy as np
import jax
import jax.numpy as jnp
from jax import lax
from jax.experimental import pallas as pl
from jax.experimental.pallas import tpu as pltpu

F32 = jnp.float32
BF16 = jnp.bfloat16

DEPTH = 4
D_MLSTM = 1024
D_HGRN = 1024
M_HEADS = 4
M_DV = 256
M_DQK = 128
H_HEADS = 8
H_DH = 128
CONV_W = 4
N_MOD = 6
EPS = 1e-5
ALPHA = (2 * DEPTH) ** 0.25
N_GATE = 128
SUBLANES = 8
BF16_ROWS = 16
CONV_HALO = SUBLANES
LOG2E = 1.4426950408889634

VMEM_LIMIT = 56 * 1024 * 1024

MLSTM_CHUNK = 256
MLSTM_BLOCK = 512
HGRN_CHUNK = 128
HGRN_BLOCK = 512

MOD_COLS = 2048
MODULATE_ROWS = 1024
IN_PROJ_TILE = (2048, 1024)
OUTPROJ_TILE = (512, 128, 512)
FFN_UP_TILE = (2048, 512, 256)
FFN_DOWN_TILE = (256, 128, 512)


def _cparams(sem):
    return pltpu.CompilerParams(dimension_semantics=sem, vmem_limit_bytes=VMEM_LIMIT)


def _sigmoid(x):
    return 1.0 / (1.0 + jnp.exp(-x))


def _log_sigmoid(x):
    return jnp.minimum(x, 0.0) - jnp.log(1.0 + jnp.exp(-jnp.abs(x)))


def _split3(x):
    hi = x.astype(BF16)
    r1 = x - hi.astype(F32)
    mid = r1.astype(BF16)
    lo = (r1 - mid.astype(F32)).astype(BF16)
    return hi, mid, lo


def _dot(a, b):
    return jnp.dot(a, b, preferred_element_type=F32)


def _dot_nt(a, b):
    return lax.dot_general(a, b, (((1,), (1,)), ((), ())), preferred_element_type=F32)


def _dot_tn(a, b):
    return lax.dot_general(a, b, (((0,), (0,)), ((), ())), preferred_element_type=F32)


def _mod_kernel(c_ref, w_ref, b_ref, o_ref):
    c = c_ref[...]
    ca = (c * _sigmoid(c)).astype(BF16)
    o_ref[0] = _dot(ca, w_ref[0].astype(BF16)) + b_ref[0]


def _mod_all(c_pad, w_mod, b_mod):
    depth, d, n = w_mod.shape
    tn = MOD_COLS
    rows = c_pad.shape[0]
    return pl.pallas_call(
        _mod_kernel,
        grid=(depth, n // tn),
        in_specs=[pl.BlockSpec((rows, d), lambda l, j: (0, 0)),
                  pl.BlockSpec((1, d, tn), lambda l, j: (l, 0, j)),
                  pl.BlockSpec((1, 1, tn), lambda l, j: (l, 0, j))],
        out_specs=pl.BlockSpec((1, rows, tn), lambda l, j: (l, 0, j)),
        out_shape=jax.ShapeDtypeStruct((depth, rows, n), F32),
        compiler_params=_cparams(("arbitrary", "arbitrary")),
        name="adaln_mod",
    )(c_pad, w_mod, b_mod.reshape(depth, 1, n))


def _modulate_kernel(x_ref, sc_ref, sh_ref, u_ref):
    u_ref[...] = (x_ref[...] * (1.0 + sc_ref[0]) + sh_ref[0]).astype(u_ref.dtype)


def _modulate(x2, sc, sh, seq):
    m, d = x2.shape
    tm = MODULATE_ROWS
    per_b = seq // tm
    vec = pl.BlockSpec((1, 1, d), lambda i: (i // per_b, 0, 0))
    return pl.pallas_call(
        _modulate_kernel,
        grid=(m // tm,),
        in_specs=[pl.BlockSpec((tm, d), lambda i: (i, 0)), vec, vec],
        out_specs=pl.BlockSpec((tm, d), lambda i: (i, 0)),
        out_shape=jax.ShapeDtypeStruct((m, d), BF16),
        compiler_params=_cparams(("arbitrary",)),
        name="modulate0",
    )(x2, sc, sh)


def _mm_kernel(x_ref, w_ref, o_ref):
    o_ref[...] = _dot(x_ref[...], w_ref[...]).astype(o_ref.dtype)


def _matmul(x, w_all, layer, out_dtype, tm, tn, name):
    m, k = x.shape
    n = w_all.shape[2]
    return pl.pallas_call(
        _mm_kernel,
        grid=(n // tn, m // tm),
        in_specs=[pl.BlockSpec((tm, k), lambda j, i: (i, 0)),
                  pl.BlockSpec((None, k, tn), lambda j, i: (layer, 0, j))],
        out_specs=pl.BlockSpec((tm, tn), lambda j, i: (i, j)),
        out_shape=jax.ShapeDtypeStruct((m, n), out_dtype),
        compiler_params=_cparams(("arbitrary", "arbitrary")),
        name=name,
    )(x, w_all)


def _mlstm_kernel(qk_ref, v_ref, og_ref, u_ref, wgc_ref, wgr_ref, cw_ref, cb_ref, bc_ref, br_ref, nw_ref,
                  tri_ref, trit_ref, y_ref, xp_ref, c_ref, n_ref, m_ref, *, L):
    @pl.when(pl.program_id(1) == 0)
    def _():
        xp_ref[0:CONV_HALO, :] = jnp.zeros((CONV_HALO, xp_ref.shape[1]), F32)
        c_ref[...] = jnp.zeros(c_ref.shape, F32)
        n_ref[...] = jnp.zeros(n_ref.shape, F32)
        m_ref[...] = jnp.zeros(m_ref.shape, F32)

    for ck in range(qk_ref.shape[0] // L):
        rows = pl.ds(ck * L, L)
        _mlstm_chunk(qk_ref.at[rows], v_ref.at[rows], og_ref.at[rows], u_ref.at[rows], wgc_ref, wgr_ref, cw_ref,
                     cb_ref, bc_ref, br_ref, nw_ref, tri_ref, trit_ref, y_ref.at[rows], xp_ref, c_ref, n_ref, m_ref,
                     L=L)


def _mlstm_chunk(qk_ref, v_ref, og_ref, u_ref, wgc_ref, wgr_ref, cw_ref, cb_ref, bc_ref, br_ref, nw_ref,
                 tri_ref, trit_ref, y_ref, xp_ref, c_ref, n_ref, m_ref, *, L):
    dqk_all = M_HEADS * M_DQK

    x = qk_ref[...]
    xp_ref[CONV_HALO:CONV_HALO + L, :] = x
    y = cb_ref[...]
    for j in range(CONV_W):
        off = CONV_HALO - (CONV_W - 1) + j
        y = y + cw_ref[j:j + 1, :] * xp_ref[off:off + L, :]
    xp_ref[0:CONV_HALO, :] = x[L - CONV_HALO:L, :]
    qk = y * _sigmoid(y)

    u = u_ref[...]
    gc = _dot(u, wgc_ref[...]) + bc_ref[...]
    gr = _dot_nt(wgr_ref[...], u)[0:2 * M_HEADS, :] + br_ref[...]
    bcum_c = _dot(tri_ref[...], jnp.concatenate(_split3(_log_sigmoid(gc) * LOG2E), axis=0))
    bcum_r = _dot(jnp.concatenate(_split3(_log_sigmoid(gr) * LOG2E), axis=1), trit_ref[...])
    gc = gc * LOG2E
    gr = gr * LOG2E

    ti = lax.broadcasted_iota(jnp.int32, (L, L), 0)
    si = lax.broadcasted_iota(jnp.int32, (L, L), 1)
    causal = si <= ti

    for h in range(M_HEADS):
        q = qk[:, h * M_DQK:(h + 1) * M_DQK] * (M_DQK ** -0.5)
        k = qk[:, dqk_all + h * M_DQK: dqk_all + (h + 1) * M_DQK]
        qb = q.astype(BF16)
        kb = k.astype(BF16)
        vb = v_ref[:, h * M_DV:(h + 1) * M_DV].astype(BF16)
        bc = bcum_c[:, M_HEADS + h:M_HEADS + h + 1]
        ic = gc[:, h:h + 1]
        br = bcum_r[M_HEADS + h:M_HEADS + h + 1, :]
        ir = gr[h:h + 1, :]
        c_prev = c_ref[h]
        n_prev = n_ref[h:h + 1, :]
        m_prev = m_ref[h:h + 1, 0:1]

        log_d = jnp.where(causal, bc - br + ir, -jnp.inf)
        m_inter = bc + m_prev
        m_t = jnp.maximum(m_inter, jnp.max(log_d, axis=1, keepdims=True))
        dw = jnp.exp2(log_d - m_t)
        w_inter = jnp.exp2(m_inter - m_t)
        s = _dot_nt(qb, kb) * dw
        num = _dot(s.astype(BF16), vb) + w_inter * _dot(qb, c_prev.astype(BF16))
        den = jnp.sum(s, axis=1, keepdims=True) + w_inter * jnp.sum(q * n_prev, axis=1, keepdims=True)
        hh = num * (1.0 / jnp.maximum(jnp.abs(den), jnp.exp2(-m_t)))

        b_last = bc[L - 1:L, :]
        lwe = b_last - bc + ic
        m_new = jnp.maximum(b_last + m_prev, jnp.max(lwe, axis=0, keepdims=True))
        w_end = jnp.exp2(lwe - m_new)
        decay = jnp.exp2(b_last + m_prev - m_new)
        kw = k * w_end
        c_ref[h] = decay * c_prev + _dot_tn(kw.astype(BF16), vb)
        n_ref[h:h + 1, :] = decay * n_prev + jnp.sum(kw, axis=0, keepdims=True)
        m_ref[h:h + 1, :] = jnp.broadcast_to(m_new, (1, m_ref.shape[1]))

        hc = hh - jnp.mean(hh, axis=1, keepdims=True)
        hn = hc * lax.rsqrt(jnp.mean(hc * hc, axis=1, keepdims=True) + EPS)
        sl = slice(h * M_DV, (h + 1) * M_DV)
        y_ref[:, sl] = (hn * nw_ref[:, sl] * _sigmoid(og_ref[:, sl])).astype(y_ref.dtype)


def _tri_consts(L):
    tri = np.tril(np.ones((L, L), np.float32))
    tri3 = np.concatenate([tri, tri, tri], axis=1)
    trit3 = np.concatenate([tri.T, tri.T, tri.T], axis=0)
    return jnp.asarray(tri3, BF16), jnp.asarray(trit3, BF16)


def _mlstm(proj, u, wgc_all, wgr_all, layer, conv_w, conv_b, b_ig, b_fg, norm_w, batch, seq):
    L = MLSTM_CHUNK
    blk = MLSTM_BLOCK
    nc = seq // blk
    wq = 2 * M_HEADS * M_DQK
    tri3, trit3 = _tri_consts(L)
    bias_c = jnp.zeros((1, N_GATE), F32).at[0, :M_HEADS].set(b_ig).at[0, M_HEADS:2 * M_HEADS].set(b_fg)
    bias_r = jnp.concatenate([b_ig, b_fg]).reshape(2 * M_HEADS, 1)
    d = u.shape[1]
    row = lambda b, c: (b * nc + c, 0)
    const = lambda b, c: (0, 0)
    return pl.pallas_call(
        functools.partial(_mlstm_kernel, L=L),
        grid=(batch, nc),
        in_specs=[pl.BlockSpec((blk, wq), lambda b, c: (b * nc + c, 0)),
                  pl.BlockSpec((blk, D_MLSTM), lambda b, c: (b * nc + c, 1)),
                  pl.BlockSpec((blk, D_MLSTM), lambda b, c: (b * nc + c, 2)),
                  pl.BlockSpec((blk, d), row),
                  pl.BlockSpec((None, d, N_GATE), lambda b, c: (layer, 0, 0)),
                  pl.BlockSpec((None, BF16_ROWS, d), lambda b, c: (layer, 0, 0)),
                  pl.BlockSpec((CONV_W, wq), const),
                  pl.BlockSpec((1, wq), const),
                  pl.BlockSpec((1, N_GATE), const),
                  pl.BlockSpec((2 * M_HEADS, 1), const),
                  pl.BlockSpec((1, D_MLSTM), const),
                  pl.BlockSpec((L, 3 * L), const),
                  pl.BlockSpec((3 * L, L), const)],
        out_specs=pl.BlockSpec((blk, D_MLSTM), row),
        out_shape=jax.ShapeDtypeStruct((batch * seq, D_MLSTM), BF16),
        scratch_shapes=[pltpu.VMEM((CONV_HALO + L, wq), F32),
                        pltpu.VMEM((M_HEADS, M_DQK, M_DV), F32),
                        pltpu.VMEM((8, M_DQK), F32),
                        pltpu.VMEM((8, 128), F32)],
        compiler_params=_cparams(("arbitrary", "arbitrary")),
        name="mlstm",
    )(proj, proj, proj, u, wgc_all, wgr_all, conv_w, conv_b.reshape(1, wq), bias_c, bias_r,
      norm_w.reshape(1, D_MLSTM), tri3, trit3)


def _hgrn_chunk(q_ref, f_ref, i_ref, g_ref, lbl_ref, nw_ref, tri_ref, y_ref, st_ref, b_ref, *, L, layer):
    dh_all = H_HEADS * H_DH
    zero = lambda n: jnp.zeros((n, dh_all), F32)

    qf = q_ref[...]
    f = f_ref[...]
    qs = qf * _sigmoid(qf)
    e = jnp.exp(-jnp.abs(f))
    d = 1.0 + e
    log_f = jnp.minimum(f, 0.0) - jnp.log(d)
    kk = jnp.where(f >= 0.0, e, 1.0) / d
    if layer > 0:
        lg = lbl_ref[...]
        ex = jnp.exp(lg - jnp.max(lg, axis=0, keepdims=True))
        pr = ex / jnp.sum(ex, axis=0, keepdims=True)
        lb = pr[1:2, :]
        for j in range(2, layer + 1):
            lb = lb + pr[j:j + 1, :]
        c = jnp.log1p(-lb) + log_f
        a = jnp.log(lb)
        log_f = jnp.maximum(a, c) + jnp.log(1.0 + jnp.exp(-jnp.abs(a - c)))
        kk = (1.0 - lb) * kk

    lf2 = log_f * LOG2E
    b = _dot(tri_ref[...], jnp.concatenate(_split3(lf2), axis=0))
    b_ref[...] = b
    b_last = b[L - 1:L, :]
    qb = (qs * jnp.exp2(b)).astype(BF16)
    kd = (kk * jnp.exp2(b_last - b)).astype(BF16)
    dec = jnp.exp2(b_last)

    sub = lax.broadcasted_iota(jnp.int32, (SUBLANES, dh_all), 0)
    groups = range(L // SUBLANES)
    grp = lambda v, g: v[g * SUBLANES:(g + 1) * SUBLANES]
    odd = (sub & 1) != 0
    f2 = jnp.exp2(lf2)
    q_lv = [qs.astype(BF16),
            jnp.concatenate([jnp.where(odd, grp(qs, g) * grp(f2, g), 0.0) for g in groups], axis=0).astype(BF16)]
    k_lv = [kk.astype(BF16),
            jnp.concatenate([jnp.where(odd, 0.0, grp(kk, g)) for g in groups], axis=0).astype(BF16)]
    half = 2
    while half < L:
        qp, kp = [], []
        if half < SUBLANES:
            upper = (sub & half) != 0
            for g in groups:
                r = None
                for mblk in range(SUBLANES // (2 * half)):
                    cand = jnp.broadcast_to(b_ref[pl.ds(g * SUBLANES + mblk * 2 * half + half - 1, 1), :],
                                            (SUBLANES, dh_all))
                    r = cand if r is None else jnp.where(sub >= mblk * 2 * half, cand, r)
                fac = jnp.exp2(-jnp.abs(grp(b, g) - r))
                qp.append(jnp.where(upper, grp(qs, g) * fac, 0.0))
                kp.append(jnp.where(upper, 0.0, grp(kk, g) * fac))
        else:
            for j in range(L // (2 * half)):
                lo = slice(j * 2 * half, j * 2 * half + half)
                up = slice(j * 2 * half + half, (j + 1) * 2 * half)
                r = b_ref[pl.ds(j * 2 * half + half - 1, 1), :]
                kp += [kk[lo] * jnp.exp2(r - b[lo]), zero(half)]
                qp += [zero(half), qs[up] * jnp.exp2(b[up] - r)]
        q_lv.append(jnp.concatenate(qp, axis=0).astype(BF16))
        k_lv.append(jnp.concatenate(kp, axis=0).astype(BF16))
        half *= 2

    ti = lax.broadcasted_iota(jnp.int32, (L, L), 0)
    si = lax.broadcasted_iota(jnp.int32, (L, L), 1)
    txb = (ti ^ si).astype(F32).astype(BF16)

    atts = []
    for h in range(H_HEADS):
        sl = slice(h * H_DH, (h + 1) * H_DH)
        att = _dot_nt(q_lv[-1][:, sl], k_lv[-1][:, sl]).astype(BF16)
        for lev in range(len(q_lv) - 2, -1, -1):
            term = _dot_nt(q_lv[lev][:, sl], k_lv[lev][:, sl]).astype(BF16)
            att = jnp.where(txb < jnp.asarray(1 << lev, BF16), term, att)
        atts.append(att)

    for h in range(H_HEADS):
        sl = slice(h * H_DH, (h + 1) * H_DH)
        vb = i_ref[:, sl].astype(BF16)
        st = st_ref[h]
        o = _dot(atts[h], vb) + _dot_nt(qb[:, sl], st.astype(BF16))
        st_ref[h] = st * dec[:, sl] + _dot_tn(vb, kd[:, sl])
        on = o * lax.rsqrt(jnp.mean(o * o, axis=1, keepdims=True) + EPS)
        g = g_ref[:, sl]
        y_ref[:, sl] = (on * nw_ref[:, sl] * (g * _sigmoid(g))).astype(y_ref.dtype)


def _hgrn_kernel(q_ref, f_ref, i_ref, g_ref, lbl_ref, nw_ref, tri_ref, y_ref, st_ref, b_ref, *, L, layer):
    @pl.when(pl.program_id(1) == 0)
    def _():
        st_ref[...] = jnp.zeros(st_ref.shape, F32)

    for ck in range(q_ref.shape[0] // L):
        rows = pl.ds(ck * L, L)
        _hgrn_chunk(q_ref.at[rows], f_ref.at[rows], i_ref.at[rows], g_ref.at[rows], lbl_ref, nw_ref, tri_ref,
                    y_ref.at[rows], st_ref, b_ref.at[rows], L=L, layer=layer)


def _hgrn(proj, lb_logits, norm_w, layer, batch, seq):
    L = HGRN_CHUNK
    rows = HGRN_BLOCK
    nc = seq // rows
    tri3, _ = _tri_consts(L)
    const = lambda b, c: (0, 0)
    col = lambda j: pl.BlockSpec((rows, D_HGRN), lambda b, c: (b * nc + c, j))
    return pl.pallas_call(
        functools.partial(_hgrn_kernel, L=L, layer=layer),
        grid=(batch, nc),
        in_specs=[col(3), col(4), col(5), col(6),
                  pl.BlockSpec(lb_logits.shape, const),
                  pl.BlockSpec((1, D_HGRN), const),
                  pl.BlockSpec(tri3.shape, const)],
        out_specs=pl.BlockSpec((rows, D_HGRN), lambda b, c: (b * nc + c, 0)),
        out_shape=jax.ShapeDtypeStruct((batch * seq, D_HGRN), BF16),
        scratch_shapes=[pltpu.VMEM((H_HEADS, H_DH, H_DH), F32), pltpu.VMEM((rows, D_HGRN), F32)],
        compiler_params=_cparams(("arbitrary", "arbitrary")),
        name="hgrn2",
    )(proj, proj, proj, proj, lb_logits, norm_w.reshape(1, D_HGRN), tri3)


def _matmul_residual_ln(ychunk, x_ref, gate_ref, lng_ref, lnb_ref, sc_ref, sh_ref, xo_ref, uo_ref, z_ref,
                        *, sub, cw):
    tm, d = z_ref.shape
    nsub, nchunk = tm // sub, d // cw
    gp = (1.0 + gate_ref[0]) * (1.0 / ALPHA)
    lng, lnb = lng_ref[...], lnb_ref[...]
    if uo_ref is not None:
        ug = lng * (1.0 + sc_ref[0])
        ub = lnb * (1.0 + sc_ref[0]) + sh_ref[0]
    mu = None
    for s in range(nsub + 1):
        rows = pl.ds(s * sub, sub)
        prev = pl.ds((s - 1) * sub, sub)
        rsum = None
        for j in range(nchunk):
            cols = slice(j * cw, (j + 1) * cw)
            if s < nsub:
                z = x_ref[rows, cols] + gp[:, cols] * ychunk(rows, cols)
                z_ref[rows, cols] = z
                part = jnp.sum(z, axis=1, keepdims=True)
                rsum = part if rsum is None else rsum + part
            if s > 0:
                if j == 0:
                    zc = z_ref[prev, :] - mu
                    rstd = lax.rsqrt(jnp.mean(zc * zc, axis=1, keepdims=True) + EPS / (ALPHA * ALPHA))
                t = (z_ref[prev, cols] - mu) * rstd
                xo_ref[prev, cols] = t * lng[:, cols] + lnb[:, cols]
                if uo_ref is not None:
                    uo_ref[prev, cols] = (t * ug[:, cols] + ub[:, cols]).astype(uo_ref.dtype)
        if s < nsub:
            mu = rsum * (1.0 / d)


def _outproj_kernel(ym_ref, yh_ref, w_ref, x_ref, gate_ref, lng_ref, lnb_ref, sc_ref, sh_ref, xo_ref, uo_ref,
                    wb_ref, z_ref, *, sub, cw):
    @pl.when(pl.program_id(0) == 0)
    def _():
        wb_ref[...] = w_ref[...].astype(BF16)

    def ychunk(rows, cols):
        return (_dot(ym_ref[rows, :], wb_ref[0:D_MLSTM, cols]) +
                _dot(yh_ref[rows, :], wb_ref[D_MLSTM:D_MLSTM + D_HGRN, cols]))

    _matmul_residual_ln(ychunk, x_ref, gate_ref, lng_ref, lnb_ref, sc_ref, sh_ref, xo_ref, uo_ref, z_ref,
                        sub=sub, cw=cw)


def _outproj_ln(ym, yh, w_all, layer, x2, gate, ln_g, ln_b, sc, sh, seq):
    m, d = x2.shape
    wshape = w_all.shape[1:]
    tm, sub, cw = OUTPROJ_TILE
    per_b = seq // tm
    vec = pl.BlockSpec((1, 1, d), lambda i: (i // per_b, 0, 0))
    par = pl.BlockSpec((1, d), lambda i: (0, 0))
    rowblk = lambda width: pl.BlockSpec((tm, width), lambda i: (i, 0))
    return pl.pallas_call(
        functools.partial(_outproj_kernel, sub=sub, cw=cw),
        grid=(m // tm,),
        in_specs=[rowblk(D_MLSTM), rowblk(D_HGRN),
                  pl.BlockSpec((None,) + wshape, lambda i: (layer, 0, 0), pipeline_mode=pl.Buffered(1)),
                  rowblk(d), vec, par, par, vec, vec],
        out_specs=[rowblk(d), rowblk(d)],
        out_shape=[jax.ShapeDtypeStruct((m, d), F32), jax.ShapeDtypeStruct((m, d), BF16)],
        scratch_shapes=[pltpu.VMEM(wshape, BF16), pltpu.VMEM((tm, d), F32)],
        compiler_params=_cparams(("arbitrary",)),
        name="outproj_ln",
    )(ym, yh, w_all, x2, gate, ln_g.reshape(1, d), ln_b.reshape(1, d), sc, sh)


def _ffn_up_kernel(u_ref, wg_ref, wu_ref, h_ref, wgb_ref, wub_ref, *, sub):
    @pl.when(pl.program_id(1) == 0)
    def _():
        wgb_ref[...] = wg_ref[...].astype(BF16)
        wub_ref[...] = wu_ref[...].astype(BF16)

    for r in range(h_ref.shape[0] // sub):
        rows = pl.ds(r * sub, sub)
        u = u_ref[rows, :]
        g = _dot(u, wgb_ref[...])
        h_ref[rows, :] = (g * _sigmoid(g) * _dot(u, wub_ref[...])).astype(h_ref.dtype)


def _ffn_up(u, wg_all, wu_all, layer):
    m, k = u.shape
    n = wg_all.shape[2]
    tm, tn, sub = FFN_UP_TILE
    wspec = pl.BlockSpec((None, k, tn), lambda j, i: (layer, 0, j))
    return pl.pallas_call(
        functools.partial(_ffn_up_kernel, sub=sub),
        grid=(n // tn, m // tm),
        in_specs=[pl.BlockSpec((tm, k), lambda j, i: (i, 0)), wspec, wspec],
        out_specs=pl.BlockSpec((tm, tn), lambda j, i: (i, j)),
        out_shape=jax.ShapeDtypeStruct((m, n), BF16),
        scratch_shapes=[pltpu.VMEM((k, tn), BF16), pltpu.VMEM((k, tn), BF16)],
        compiler_params=_cparams(("arbitrary", "arbitrary")),
        name="ffn_up",
    )(u, wg_all, wu_all)


def _down_kernel(h_ref, w_ref, x_ref, gate_ref, lng_ref, lnb_ref, sc_ref, sh_ref, *refs, sub, cw, emit_u):
    xo_ref = refs[0]
    uo_ref = refs[1] if emit_u else None
    z_ref = refs[-1]
    ychunk = lambda rows, cols: _dot(h_ref[rows, :], w_ref[:, cols])
    _matmul_residual_ln(ychunk, x_ref, gate_ref, lng_ref, lnb_ref, sc_ref, sh_ref, xo_ref, uo_ref, z_ref,
                        sub=sub, cw=cw)


def _down_ln(hid, w_all, layer, x2, gate, ln_g, ln_b, sc, sh, seq, emit_u):
    m, d = x2.shape
    kdim = hid.shape[1]
    tm, sub, cw = FFN_DOWN_TILE
    per_b = seq // tm
    vec = pl.BlockSpec((1, 1, d), lambda i: (i // per_b, 0, 0))
    par = pl.BlockSpec((1, d), lambda i: (0, 0))
    rowblk = pl.BlockSpec((tm, d), lambda i: (i, 0))
    n_out = 2 if emit_u else 1
    return pl.pallas_call(
        functools.partial(_down_kernel, sub=sub, cw=cw, emit_u=emit_u),
        grid=(m // tm,),
        in_specs=[pl.BlockSpec((tm, kdim), lambda i: (i, 0)),
                  pl.BlockSpec((None,) + w_all.shape[1:], lambda i: (layer, 0, 0), pipeline_mode=pl.Buffered(1)),
                  rowblk, vec, par, par, vec, vec],
        out_specs=[rowblk, rowblk][:n_out],
        out_shape=[jax.ShapeDtypeStruct((m, d), F32), jax.ShapeDtypeStruct((m, d), BF16)][:n_out],
        scratch_shapes=[pltpu.VMEM((tm, d), F32)],
        compiler_params=_cparams(("arbitrary",)),
        name="ffn_down_ln",
    )(hid, w_all, x2, gate, ln_g.reshape(1, d), ln_b.reshape(1, d), sc, sh)


def _prep_w_in(w_in):
    g0 = 3 * 1024
    g1 = g0 + 2 * M_HEADS
    big = jnp.concatenate([w_in[:, :, :g0], w_in[:, :, g1:]], axis=2).astype(BF16)
    gates = w_in[:, :, g0:g1]
    wgc = jnp.pad(gates, ((0, 0), (0, 0), (0, N_GATE - 2 * M_HEADS))).astype(BF16)
    wgr = jnp.pad(jnp.swapaxes(gates, 1, 2), ((0, 0), (0, BF16_ROWS - 2 * M_HEADS), (0, 0))).astype(BF16)
    return big, wgc, wgr


def kernel(x, c, w_mod, b_mod, w_in, conv_w, conv_b, b_igate, b_fgate, mlstm_norm_w, hgrn_norm_w, lb_logits, w_out, ln1_g, ln1_b, w_gate, w_up, w_down, ln2_g, ln2_b):
    batch, seq, d = x.shape
    depth = w_mod.shape[0]
    m = batch * seq
    x2 = x.reshape(m, d)

    c_pad = jnp.pad(c, ((0, 8 - batch), (0, 0)))
    mod = _mod_all(c_pad, w_mod, b_mod)[:, :batch]
    mod = mod.reshape(depth, batch, N_MOD, 1, d)
    mvec = lambda l, k: mod[l, :, k]

    w_big, wgc, wgr = _prep_w_in(w_in)
    w_down_bf = w_down.astype(BF16)
    u = _modulate(x2, mvec(0, 1), mvec(0, 0), seq)
    for l in range(depth):
        proj = _matmul(u, w_big, l, F32, *IN_PROJ_TILE, "in_proj")
        ym = _mlstm(proj, u, wgc, wgr, l, conv_w[l], conv_b[l], b_igate[l], b_fgate[l], mlstm_norm_w[l],
                    batch, seq)
        yh = _hgrn(proj, lb_logits, hgrn_norm_w[l], l, batch, seq)
        x2, u = _outproj_ln(ym, yh, w_out, l, x2, mvec(l, 2), ln1_g[l], ln1_b[l], mvec(l, 4), mvec(l, 3), seq)
        hid = _ffn_up(u, w_gate, w_up, l)
        last = l == depth - 1
        nl = l if last else l + 1
        outs = _down_ln(hid, w_down_bf, l, x2, mvec(l, 5), ln2_g[l], ln2_b[l],
                        mvec(nl, 1), mvec(nl, 0), seq, not last)
        x2 = outs[0]
        u = None if last else outs[1]
    return x2.reshape(batch, seq, d)
```
